```python
import jax, jax.numpy as jnp
from jax import lax
import numpy as np

D_MODEL = 1024
BATCH = 4
SEQ = 8192
DEPTH = 1
DEC_BATCH = 128
DEC_SEQ = 1
PAST_LEN = 8192
PAGE_SIZE = 128

H_R = 4
DK_R = 128
DV_R = 256
RET_CHUNK = 128
ROPE_BASE = 10000.0
H_F = 8
DH_F = 64
Q_BLOCK = 128
D_FF = 2816
EPS = 1e-6
N_SUB = 3
NEG_INF = -1e30

D_RQK = H_R * DK_R
D_RV = H_R * DV_R
D_FQK = H_F * DH_F
D_IN = 2 * D_RQK + 2 * D_RV + 3 * D_FQK + H_F + 2 * D_MODEL
D_MIX = D_RV + D_FQK

kernel_name = 'retnet_fox_macaron_adaln_decode_step'


def rms_norm(x):
    xf = x.astype(jnp.float32)
    return (xf * lax.rsqrt(jnp.mean(xf * xf, -1, keepdims=True) + EPS)).astype(x.dtype)


def head_rms(x, w):
    return rms_norm(x) * w


def head_group_norm(o, w):
    of = o.astype(jnp.float32)
    mu = jnp.mean(of, -1, keepdims=True)
    var = jnp.mean(jnp.square(of - mu), -1, keepdims=True)
    on = ((of - mu) * lax.rsqrt(var + EPS)).astype(o.dtype)
    return on.reshape(o.shape[0], o.shape[1], -1) * w


def adaln(c, w_ada, b_ada):
    m = jax.nn.silu(c) @ w_ada + b_ada
    m = m.reshape(c.shape[0], N_SUB, 3, D_MODEL)
    return jnp.transpose(m, (1, 2, 0, 3))[:, :, :, None, :]


def modulate(x, shift, scale):
    return rms_norm(x) * (1.0 + scale) + shift


def swiglu(h, w_up, w_down):
    g, u = jnp.split(h @ w_up, 2, axis=-1)
    return (jax.nn.silu(g) * u) @ w_down


def rope(x, pos):
    half = x.shape[-1] // 2
    inv = ROPE_BASE ** (-jnp.arange(half, dtype=jnp.float32) / half)
    ang = pos.astype(jnp.float32)[:, None] * inv[None, :]
    cos = jnp.cos(ang)[None, :, None, :].astype(x.dtype)
    sin = jnp.sin(ang)[None, :, None, :].astype(x.dtype)
    x1, x2 = x[..., :half], x[..., half:]
    return jnp.concatenate([x1 * cos - x2 * sin, x2 * cos + x1 * sin], axis=-1)


def retention_log_gamma():
    return jnp.log(1.0 - 2.0 ** (-5.0 - jnp.arange(H_R, dtype=jnp.float32)))


def retention_chunk(s0, qkv):
    q, k, v = qkv
    L = q.shape[1]
    lg = retention_log_gamma()
    i = jnp.arange(L, dtype=jnp.float32)
    diff = i[:, None] - i[None, :]
    decay = jnp.where(diff >= 0, jnp.exp(lg[:, None, None] * jnp.maximum(diff, 0.0)), 0.0)
    scores = jnp.einsum('bihd,bjhd->bhij', q, k) * decay.astype(q.dtype)
    o = jnp.einsum('bhij,bjhv->bihv', scores, v)
    q_decay = jnp.exp(lg[None, :] * (i[:, None] + 1.0))
    o = o + jnp.einsum('bihd,bhdv->bihv', q, s0) * q_decay[None, :, :, None].astype(q.dtype)
    k_decay = jnp.exp(lg[None, :] * (L - 1.0 - i)[:, None])
    s1 = s0 * jnp.exp(lg * L)[None, :, None, None].astype(s0.dtype) + jnp.einsum(
        'bjhd,bjhv->bhdv', k * k_decay[None, :, :, None].astype(k.dtype), v)
    return s1, o


def fox_attend(q, k, v, c_q, c_k, q_pos, k_pos):
    s = jnp.einsum('bqhd,bkhd->bhqk', q, k).astype(jnp.float32) * (DH_F ** -0.5)
    s = s + jnp.transpose(c_q, (0, 2, 1))[:, :, :, None] - jnp.transpose(c_k, (0, 2, 1))[:, :, None, :]
    s = jnp.where(k_pos[None, :] <= q_pos[:, None], s, NEG_INF)
    p = jax.nn.softmax(s, axis=-1).astype(v.dtype)
    return jnp.einsum('bhqk,bkhd->bqhd', p, v)


def split_projection(z):
    sizes = (D_RQK, D_RQK, D_RV, D_RV, D_FQK, D_FQK, D_FQK, H_F, D_MODEL, D_MODEL)
    points = [int(p) for p in np.cumsum(sizes)[:-1]]
    return jnp.split(z, points, axis=-1)


def token_mixer(h, pos, ret_s0, past, w_in, b_forget, q_norm_w, k_norm_w, ret_norm_w, w_branch, w_out):
    B, T, _ = h.shape
    rq, rk, rv, rg, fq, fk, fv, fl, gr, gf = split_projection(h @ w_in)
    rq = rope(rq.reshape(B, T, H_R, DK_R), pos)
    rk = rope(rk.reshape(B, T, H_R, DK_R), pos) * (DK_R ** -0.5)
    rv = rv.reshape(B, T, H_R, DV_R)
    if ret_s0 is None:
        n = T // RET_CHUNK
        to_chunks = lambda a: jnp.moveaxis(a.reshape(B, n, RET_CHUNK, *a.shape[2:]), 1, 0)
        s0 = jnp.zeros((B, H_R, DK_R, DV_R), rv.dtype)
        ret_state, o_r = lax.scan(retention_chunk, s0, (to_chunks(rq), to_chunks(rk), to_chunks(rv)))
        o_r = jnp.moveaxis(o_r, 0, 1).reshape(B, T, H_R, DV_R)
    else:
        ret_state, o_r = retention_chunk(ret_s0.astype(rv.dtype), (rq, rk, rv))
    o_r = jax.nn.silu(rg) * head_group_norm(o_r, ret_norm_w)
    fq = head_rms(fq.reshape(B, T, H_F, DH_F), q_norm_w)
    fk = head_rms(fk.reshape(B, T, H_F, DH_F), k_norm_w)
    fv = fv.reshape(B, T, H_F, DH_F)
    logf = jax.nn.log_sigmoid(fl.astype(jnp.float32) + b_forget.astype(jnp.float32))
    if past is None:
        c = jnp.cumsum(logf, axis=1)
        k_pos = jnp.arange(T)

        def q_block(i):
            start = i * Q_BLOCK
            qb = lax.dynamic_slice_in_dim(fq, start, Q_BLOCK, axis=1)
            cb = lax.dynamic_slice_in_dim(c, start, Q_BLOCK, axis=1)
            return fox_attend(qb, fk, fv, cb, c, start + jnp.arange(Q_BLOCK), k_pos)

        o_f = lax.map(q_block, jnp.arange(T // Q_BLOCK))
        o_f = jnp.moveaxis(o_f, 0, 1).reshape(B, T, D_FQK)
    else:
        pk, pv, plogf = past
        P = pk.shape[1]
        k_all = jnp.concatenate([pk.astype(fk.dtype), fk], axis=1)
        v_all = jnp.concatenate([pv.astype(fv.dtype), fv], axis=1)
        c_all = jnp.cumsum(jnp.concatenate([plogf.astype(jnp.float32), logf], axis=1), axis=1)
        o_f = fox_attend(fq, k_all, v_all, c_all[:, P:], c_all, P + jnp.arange(T), jnp.arange(P + T))
        o_f = o_f.reshape(B, T, D_FQK)
    merged = jax.nn.sigmoid(gr) * (o_r @ w_branch[:D_RV]) + jax.nn.sigmoid(gf) * (o_f @ w_branch[D_RV:])
    return merged @ w_out, (ret_state, fk, fv, logf)


def decoder_layer(x, c, pos, ret_s0, past, w_ada, b_ada, w_ffa_up, w_ffa_down, w_in, b_forget,
                  q_norm_w, k_norm_w, ret_norm_w, w_branch, w_out, w_ffb_up, w_ffb_down):
    mod = adaln(c, w_ada, b_ada)
    h = modulate(x, mod[0, 0], mod[0, 1])
    x = x + 0.5 * mod[0, 2] * swiglu(h, w_ffa_up, w_ffa_down)
    h = modulate(x, mod[1, 0], mod[1, 1])
    y, new_state = token_mixer(h, pos, ret_s0, past, w_in, b_forget, q_norm_w, k_norm_w,
                               ret_norm_w, w_branch, w_out)
    x = x + mod[1, 2] * y
    h = modulate(x, mod[2, 0], mod[2, 1])
    x = x + 0.5 * mod[2, 2] * swiglu(h, w_ffb_up, w_ffb_down)
    return x, new_state


def setup_inputs(seed: int = 0) -> dict:
    key = jax.random.key(seed)
    ks = jax.random.split(key, 24)
    n_pages = PAST_LEN // PAGE_SIZE
    n_used = DEC_BATCH * n_pages
    n_phys = (n_used * 5 + 3) // 4
    nrm = lambda k, shape, s=1.0: s * jax.random.normal(k, shape, jnp.float32)
    page_table = jax.random.permutation(ks[0], n_phys)[:n_used].reshape(DEC_BATCH, n_pages).astype(jnp.int32)
    return {
        'x_prompt': nrm(ks[1], (BATCH, SEQ, D_MODEL)),
        'x_sample': nrm(ks[2], (DEC_BATCH, DEC_SEQ, D_MODEL)),
        'cache_k': nrm(ks[3], (DEPTH, n_phys, PAGE_SIZE, H_F, DH_F)),
        'cache_v': nrm(ks[4], (DEPTH, n_phys, PAGE_SIZE, H_F, DH_F)),
        'cache_logf': jax.nn.log_sigmoid(2.0 + nrm(ks[5], (DEPTH, n_phys, PAGE_SIZE, H_F), 0.5)),
        'state_ret': nrm(ks[6], (DEPTH, DEC_BATCH, H_R, DK_R, DV_R), 0.1),
        'page_table': page_table,
        'c_prompt': nrm(ks[7], (BATCH, D_MODEL)),
        'c_sample': nrm(ks[8], (DEC_BATCH, D_MODEL)),
        'w_ada': nrm(ks[9], (DEPTH, D_MODEL, N_SUB * 3 * D_MODEL), D_MODEL ** -0.5),
        'b_ada': nrm(ks[10], (DEPTH, N_SUB * 3 * D_MODEL), 0.02),
        'w_ffa_up': nrm(ks[11], (DEPTH, D_MODEL, 2 * D_FF), D_MODEL ** -0.5),
        'w_ffa_down': nrm(ks[12], (DEPTH, D_FF, D_MODEL), D_FF ** -0.5),
        'w_in': nrm(ks[13], (DEPTH, D_MODEL, D_IN), D_MODEL ** -0.5),
        'b_forget': 2.0 + nrm(ks[14], (DEPTH, H_F), 0.5),
        'q_norm_w': 1.0 + nrm(ks[15], (DEPTH, DH_F), 0.05),
        'k_norm_w': 1.0 + nrm(ks[16], (DEPTH, DH_F), 0.05),
        'ret_norm_w': 1.0 + nrm(ks[17], (DEPTH, D_RV), 0.05),
        'w_branch': nrm(ks[18], (DEPTH, D_MIX, D_MODEL), D_MIX ** -0.5),
        'w_out': nrm(ks[19], (DEPTH, D_MODEL, D_MODEL), D_MODEL ** -0.5),
        'w_ffb_up': nrm(ks[20], (DEPTH, D_MODEL, 2 * D_FF), D_MODEL ** -0.5),
        'w_ffb_down': nrm(ks[21], (DEPTH, D_FF, D_MODEL), D_FF ** -0.5),
    }


def reference(x_prompt, x_sample, cache_k, cache_v, cache_logf, state_ret, page_table, c_prompt, c_sample,
              w_ada, b_ada, w_ffa_up, w_ffa_down, w_in, b_forget, q_norm_w, k_norm_w, ret_norm_w,
              w_branch, w_out, w_ffb_up, w_ffb_down):
    pos_p = jnp.arange(x_prompt.shape[1], dtype=jnp.int32)
    pos_s = PAST_LEN + jnp.arange(x_sample.shape[1], dtype=jnp.int32)
    n_seq = page_table.shape[0]
    y_prompt, y_sample = x_prompt, x_sample
    kp, vp, fp, sp, ksm, vsm, fsm, ssm = [], [], [], [], [], [], [], []
    for l in range(DEPTH):
        w = (w_ada[l], b_ada[l], w_ffa_up[l], w_ffa_down[l], w_in[l], b_forget[l], q_norm_w[l],
             k_norm_w[l], ret_norm_w[l], w_branch[l], w_out[l], w_ffb_up[l], w_ffb_down[l])
        y_prompt, (s_p, k_p, v_p, f_p) = decoder_layer(y_prompt, c_prompt, pos_p, None, None, *w)
        past = (cache_k[l, page_table].reshape(n_seq, -1, H_F, DH_F),
                cache_v[l, page_table].reshape(n_seq, -1, H_F, DH_F),
                cache_logf[l, page_table].reshape(n_seq, -1, H_F))
        y_sample, (s_s, k_s, v_s, f_s) = decoder_layer(y_sample, c_sample, pos_s, state_ret[l], past, *w)
        kp.append(k_p); vp.append(v_p); fp.append(f_p); sp.append(s_p)
        ksm.append(k_s); vsm.append(v_s); fsm.append(f_s); ssm.append(s_s)
    k_prompt, v_prompt, logf_prompt, ret_state_prompt = jnp.stack(kp), jnp.stack(vp), jnp.stack(fp), jnp.stack(sp)
    k_sample, v_sample, logf_sample, ret_state_sample = jnp.stack(ksm), jnp.stack(vsm), jnp.stack(fsm), jnp.stack(ssm)
    return (y_prompt, y_sample, k_prompt, v_prompt, logf_prompt, ret_state_prompt,
            k_sample, v_sample, logf_sample, ret_state_sample)
```

```python
import functools
import math

import jax
import jax.numpy as jnp
import numpy as np
from jax import lax
from jax.experimental import pallas as pl
from jax.experimental.pallas import tpu as pltpu

D_MODEL = 1024
PAGE_SIZE = 128
H_R, DK_R, DV_R = 4, 128, 256
RET_CHUNK = 128
ROPE_BASE = 10000.0
H_F, DH_F = 8, 64
D_FF = 2816
EPS = 1e-6
N_SUB = 3
NEG_INF = -1e30

D_RQK = H_R * DK_R
D_RV = H_R * DV_R
D_FQK = H_F * DH_F
N_PAIR = H_F // 2
LANE = 128
SUBLANE = 8

F32 = jnp.float32
BF16 = jnp.bfloat16

_SEG = {}
_off = 0
for _name, _w in (("rq", D_RQK), ("rk", D_RQK), ("rv", D_RV), ("rg", D_RV), ("fq", D_FQK), ("fk", D_FQK),
                  ("fv", D_FQK), ("gr", D_MODEL), ("gf", D_MODEL), ("fl", LANE)):
    _SEG[_name] = (_off, _off + _w)
    _off += _w
D_IN_PAD = _off

ROW_TILE = 512
ATT_BLK = 512
FF_CHUNK = D_FF // 2
RET_BLOCK = 512
DEC_PAGES = 16
DEC_SEQS = 4
VMEM_LIMIT = 56 * 2 ** 20


def _cparams(sem):
    return pltpu.CompilerParams(dimension_semantics=sem, vmem_limit_bytes=VMEM_LIMIT)


def _const_spec(shape):
    nd = len(shape)
    return pl.BlockSpec(shape, lambda *_: (0,) * nd, pipeline_mode=pl.Buffered(1))


def _dot(a, b):
    return jnp.dot(a, b, preferred_element_type=F32)


def _dot_nt(a, b):
    return lax.dot_general(a, b, (((1,), (1,)), ((), ())), preferred_element_type=F32)


def _dot_tn(a, b):
    return lax.dot_general(a, b, (((0,), (0,)), ((), ())), preferred_element_type=F32)


def _split3(x):
    hi = x.astype(BF16)
    r1 = x - hi.astype(F32)
    mid = r1.astype(BF16)
    lo = (r1 - mid.astype(F32)).astype(BF16)
    return hi, mid, lo


def _dot3_rhs(a_bf16, x):
    hi, mid, lo = _split3(x)
    return _dot(a_bf16, hi) + _dot(a_bf16, mid) + _dot(a_bf16, lo)


def _dot3_lhs(x, b_bf16):
    hi, mid, lo = _split3(x)
    return _dot(hi, b_bf16) + _dot(mid, b_bf16) + _dot(lo, b_bf16)


def _log_sigmoid(x):
    return jnp.minimum(x, 0.0) - jnp.log1p(jnp.exp(-jnp.abs(x)))


def _mod_rows(ref, per_row):
    if per_row:
        return ref[0]
    return ref[0, pl.ds(pl.program_id(0), 1), :]


def _modulated_norm(x, shift, scale):
    ms = jnp.mean(x * x, axis=-1, keepdims=True)
    return (x * lax.rsqrt(ms + EPS)) * (1.0 + scale) + shift


def _mod_specs(sub, per_row, n_rows, row_off):
    specs = []
    for k in range(3):
        j = sub * 3 + k
        if per_row:
            specs.append(pl.BlockSpec((1, n_rows, D_MODEL), lambda b, t, j=j: (j, 0, 0)))
        else:
            specs.append(pl.BlockSpec((1, SUBLANE, D_MODEL), lambda b, t, j=j: (j, row_off // SUBLANE, 0)))
    return specs


def _adaln_body(c_ref, w_ref, b_ref, o_ref):
    c = c_ref[...]
    s = c * jax.nn.sigmoid(c)
    o_ref[0] = _dot(s.astype(BF16), w_ref[...]) + b_ref[...]


def _adaln(c_all, w_ada, b_ada):
    rows = c_all.shape[0]
    n = N_SUB * 3
    return pl.pallas_call(
        _adaln_body,
        out_shape=jax.ShapeDtypeStruct((n, rows, D_MODEL), F32),
        grid=(n,),
        in_specs=[pl.BlockSpec((rows, D_MODEL), lambda j: (0, 0)),
                  pl.BlockSpec((D_MODEL, D_MODEL), lambda j: (0, j)),
                  pl.BlockSpec((1, D_MODEL), lambda j: (0, j))],
        out_specs=pl.BlockSpec((1, rows, D_MODEL), lambda j: (j, 0, 0)),
        compiler_params=_cparams(("parallel",)),
        name="adaln",
    )(c_all, w_ada, b_ada.reshape(1, -1))


def _ffn_body(x_ref, sh_ref, sc_ref, gt_ref, wup_ref, wdn_ref, o_ref, *, per_row):
    x = x_ref[0]
    sh, sc, gt = (_mod_rows(r, per_row) for r in (sh_ref, sc_ref, gt_ref))
    h = _modulated_norm(x, sh, sc).astype(BF16)
    acc = None
    for c in range(D_FF // FF_CHUNK):
        lo = c * FF_CHUNK
        g = _dot(h, wup_ref[:, lo:lo + FF_CHUNK])
        u = _dot(h, wup_ref[:, D_FF + lo:D_FF + lo + FF_CHUNK])
        a = (g * jax.nn.sigmoid(g) * u).astype(BF16)
        part = _dot(a, wdn_ref[lo:lo + FF_CHUNK, :])
        acc = part if acc is None else acc + part
    o_ref[0] = x + (0.5 * gt) * acc


def _ffn(x, mod, sub, per_row, row_off, w_up, w_dn):
    nb, nt, _ = x.shape
    tm = min(ROW_TILE, nt)
    x_spec = pl.BlockSpec((1, tm, D_MODEL), lambda b, t: (b, t, 0))
    return pl.pallas_call(
        functools.partial(_ffn_body, per_row=per_row),
        out_shape=jax.ShapeDtypeStruct(x.shape, F32),
        grid=(nb, nt // tm),
        in_specs=[x_spec, *_mod_specs(sub, per_row, nt, row_off),
                  _const_spec((D_MODEL, 2 * D_FF)), _const_spec((D_FF, D_MODEL))],
        out_specs=x_spec,
        compiler_params=_cparams(("parallel", "parallel")),
        name="ffn",
    )(x, mod, mod, mod, w_up, w_dn)


def _inproj_body(x_ref, sh_ref, sc_ref, cos_ref, sin_ref, w_ref, bf_ref, qn_ref, kn_ref, bd_ref,
                 rq_o, rk_o, rv_o, rg_o, fq_o, fkb_o, fvb_o, fk_o, fv_o, gr_o, gf_o, lfc_o, lfp_o, *, per_row):
    x = x_ref[0]
    sh, sc = (_mod_rows(r, per_row) for r in (sh_ref, sc_ref))
    h = _modulated_norm(x, sh, sc).astype(BF16)

    def seg(name):
        lo, hi = _SEG[name]
        return _dot(h, w_ref[:, lo:hi])

    cos, sin = cos_ref[...], sin_ref[...]

    def rope_store(z, out, scale):
        for hd in range(H_R):
            zh = z[:, hd * DK_R:(hd + 1) * DK_R]
            r = zh * cos + pltpu.roll(zh, DK_R // 2, 1) * sin
            if scale != 1.0:
                r = r * scale
            out[0, :, hd * DK_R:(hd + 1) * DK_R] = r.astype(out.dtype)

    rope_store(seg("rq"), rq_o, 1.0)
    rope_store(seg("rk"), rk_o, DK_R ** -0.5)
    rv_o[0] = seg("rv").astype(BF16)
    rg_o[0] = seg("rg").astype(BF16)

    bd = bd_ref[...]

    def head_rms(z, w):
        z2 = z * z
        hi = z2.astype(BF16)
        lo = (z2 - hi.astype(F32)).astype(BF16)
        ms = _dot(hi, bd) + _dot(lo, bd)
        return (z * lax.rsqrt(ms + EPS)) * w

    fq_o[0] = (head_rms(seg("fq"), qn_ref[...]) * (DH_F ** -0.5)).astype(BF16)
    fk = head_rms(seg("fk"), kn_ref[...])
    fk_o[0] = fk
    fkb_o[0] = fk.astype(BF16)
    fv = seg("fv")
    fv_o[0] = fv
    fvb_o[0] = fv.astype(BF16)
    gr_o[0] = jax.nn.sigmoid(seg("gr")).astype(BF16)
    gf_o[0] = jax.nn.sigmoid(seg("gf")).astype(BF16)

    logf = _log_sigmoid(seg("fl") + bf_ref[...])
    lane = lax.broadcasted_iota(jnp.int32, logf.shape, 1)
    lfp_o[0] = jnp.where(lane < H_F, logf, 0.0)
    lfc_o[0] = logf[:, :H_F]


def _inproj(x, mod, per_row, row_off, cos_t, sin_t, w_in, bf_row, qn, kn, bd):
    nb, nt, _ = x.shape
    tm = min(ROW_TILE, nt)
    row = lambda w: pl.BlockSpec((1, tm, w), lambda b, t: (b, t, 0))
    shp = lambda w, dt: jax.ShapeDtypeStruct((nb, nt, w), dt)
    outs = [("rq", D_RQK, BF16), ("rk", D_RQK, BF16), ("rv", D_RV, BF16), ("rg", D_RV, BF16),
            ("fq", D_FQK, BF16), ("fkb", D_FQK, BF16), ("fvb", D_FQK, BF16), ("fk", D_FQK, F32),
            ("fv", D_FQK, F32), ("gr", D_MODEL, BF16), ("gf", D_MODEL, BF16), ("lfc", H_F, F32),
            ("lfp", LANE, F32)]
    res = pl.pallas_call(
        functools.partial(_inproj_body, per_row=per_row),
        out_shape=[shp(w, dt) for _, w, dt in outs],
        grid=(nb, nt // tm),
        in_specs=[row(D_MODEL), *_mod_specs(1, per_row, nt, row_off)[:2],
                  pl.BlockSpec((tm, DK_R), lambda b, t: (t, 0)), pl.BlockSpec((tm, DK_R), lambda b, t: (t, 0)),
                  _const_spec((D_MODEL, D_IN_PAD)), _const_spec((1, LANE)), _const_spec((1, D_FQK)),
                  _const_spec((1, D_FQK)), _const_spec((D_FQK, D_FQK))],
        out_specs=[row(w) for _, w, _ in outs],
        compiler_params=_cparams(("parallel", "parallel")),
        name="inproj",
    )(x, mod, mod, cos_t, sin_t, w_in, bf_row, qn, kn, bd)
    return {name: r for (name, _, _), r in zip(outs, res)}


def _forget_terms_body(lf_ref, ltri_ref, ones_ref, pq_ref, pk_ref, qrow_ref, krow_ref,
                       qa_o, ka_o, base_o, carry):
    i = pl.program_id(1)

    @pl.when(i == 0)
    def _():
        carry[...] = jnp.zeros_like(carry)

    ltri = ltri_ref[...]
    blk = lf_ref.shape[1]
    run = jnp.zeros((1, LANE), F32)
    for r in range(blk // LANE):
        rows = slice(r * LANE, (r + 1) * LANE)
        c = _dot3_rhs(ltri, lf_ref[0, rows, :]) + run
        run = c[LANE - 1:LANE, :]
        hi, mid, lo = _split3(c)
        qa = _dot(hi, pq_ref[0]) + _dot(mid, pq_ref[1]) + _dot(lo, pq_ref[2]) + qrow_ref[...]
        ka = _dot(hi, pk_ref[0]) + _dot(mid, pk_ref[1]) + _dot(lo, pk_ref[2]) + krow_ref[...]
        qa_o[0, rows, :] = qa.astype(BF16)
        ka_o[0, rows, :] = ka.astype(BF16)

    base = carry[...]
    rr = lax.broadcasted_iota(jnp.int32, (2 * SUBLANE, LANE), 0)
    cc = lax.broadcasted_iota(jnp.int32, (2 * SUBLANE, LANE), 1)
    diag = jnp.where(rr == cc, jnp.broadcast_to(base, (2 * SUBLANE, LANE)), 0.0)
    base_o[0, 0] = _dot3_lhs(diag, ones_ref[...])[:SUBLANE]
    carry[...] = base + run


def _forget_tables():
    w = N_PAIR * LANE
    pq, pk = np.zeros((3, LANE, w), np.float32), np.zeros((3, LANE, w), np.float32)
    qrow, krow = np.zeros((1, w), np.float32), np.zeros((1, w), np.float32)
    for h in range(H_F):
        l0 = (h // 2) * LANE + (h % 2) * DH_F
        for s in range(3):
            pq[s, h, l0 + s] = 1.0
            pk[s, h, l0 + 3 + s] = -1.0
            qrow[0, l0 + 3 + s] = 1.0
            krow[0, l0 + s] = 1.0
    r = np.arange(LANE)
    ltri = (r[None, :] <= r[:, None]).astype(np.float32)
    as_bf16 = lambda a: jnp.asarray(a, dtype=BF16)
    return (as_bf16(ltri), jnp.ones((LANE, LANE), BF16), as_bf16(pq), as_bf16(pk), jnp.asarray(qrow),
            jnp.asarray(krow))


def _forget_terms(lfp, blk):
    nb, nt, _ = lfp.shape
    nblk = nt // blk
    w = N_PAIR * LANE
    ltri, ones, pq, pk, qrow, krow = _forget_tables()
    return pl.pallas_call(
        _forget_terms_body,
        out_shape=[jax.ShapeDtypeStruct((nb, nt, w), BF16), jax.ShapeDtypeStruct((nb, nt, w), BF16),
                   jax.ShapeDtypeStruct((nb, nblk, SUBLANE, LANE), F32)],
        grid=(nb, nblk),
        in_specs=[pl.BlockSpec((1, blk, LANE), lambda b, i: (b, i, 0)),
                  _const_spec((LANE, LANE)), _const_spec((LANE, LANE)), _const_spec((3, LANE, w)),
                  _const_spec((3, LANE, w)), _const_spec((1, w)), _const_spec((1, w))],
        out_specs=[pl.BlockSpec((1, blk, w), lambda b, i: (b, i, 0)),
                   pl.BlockSpec((1, blk, w), lambda b, i: (b, i, 0)),
                   pl.BlockSpec((1, 1, SUBLANE, LANE), lambda b, i: (b, i, 0, 0))],
        scratch_shapes=[pltpu.VMEM((1, LANE), F32)],
        compiler_params=_cparams(("parallel", "arbitrary")),
        name="forget_terms",
    )(lfp, ltri, ones, pq, pk, qrow, krow)


def _fox_prompt_body(q_ref, qa_ref, k_ref, ka_ref, v_ref, base_ref, o_ref, m_scr, l_scr, acc_scr):
    hp = pl.program_id(1)
    i = pl.program_id(2)
    tq = q_ref.shape[1]
    tk = tq
    lane2 = lax.broadcasted_iota(jnp.int32, (1, 2 * LANE), 1)
    first = (lane2 % LANE) < DH_F
    qf = jnp.concatenate([q_ref[0], qa_ref[0]], axis=1)
    zero = jnp.zeros_like(qf)
    q_heads = (jnp.where(first, qf, zero), jnp.where(first, zero, qf))
    base_i = [base_ref[0, i, pl.ds(2 * hp + hh, 1), :] for hh in range(2)]

    m_scr[...] = jnp.full(m_scr.shape, NEG_INF, F32)
    l_scr[...] = jnp.zeros_like(l_scr)
    acc_scr[...] = jnp.zeros_like(acc_scr)

    def step(j, masked):
        off = pl.multiple_of(j * tk, tk)
        kf = jnp.concatenate([k_ref[0, pl.ds(off, tk), :], ka_ref[0, pl.ds(off, tk), :]], axis=1)
        vv = v_ref[0, pl.ds(off, tk), :]
        for hh in range(2):
            s = _dot_nt(q_heads[hh], kf)
            if masked:
                row = lax.broadcasted_iota(jnp.int32, (tq, tk), 0)
                col = lax.broadcasted_iota(jnp.int32, (tq, tk), 1)
                s = jnp.where(col <= row, s, NEG_INF)
            d = base_i[hh] - base_ref[0, j, pl.ds(2 * hp + hh, 1), :]
            m_prev = m_scr[hh]
            m_next = jnp.maximum(m_prev, jnp.max(s, axis=1, keepdims=True) + d)
            alpha = jnp.exp(m_prev - m_next)
            sub = m_next - d
            p = jnp.exp(s - jnp.concatenate([sub] * (tk // LANE), axis=1))
            l_scr[hh] = alpha * l_scr[hh] + jnp.sum(p, axis=1, keepdims=True)
            acc_scr[hh] = alpha * acc_scr[hh] + _dot(p.astype(BF16), vv)
            m_scr[hh] = m_next

    def full_step(j, carry):
        step(j, False)
        return carry

    lax.fori_loop(0, i, full_step, 0)
    step(i, True)

    lane = lax.broadcasted_iota(jnp.int32, (1, LANE), 1)
    o = jnp.where(lane < DH_F, acc_scr[0] / l_scr[0], acc_scr[1] / l_scr[1])
    o_ref[0] = o.astype(o_ref.dtype)


def _fox_prompt(fq, qa, fkb, ka, fvb, base, blk):
    nb, nt, _ = fq.shape
    nblk = nt // blk
    qspec = pl.BlockSpec((1, blk, LANE), lambda b, p, i: (b, i, p))
    kspec = pl.BlockSpec((1, nt, LANE), lambda b, p, i: (b, 0, p))
    return pl.pallas_call(
        _fox_prompt_body,
        out_shape=jax.ShapeDtypeStruct((nb, nt, D_FQK), BF16),
        grid=(nb, N_PAIR, nblk),
        in_specs=[qspec, qspec, kspec, kspec, kspec,
                  pl.BlockSpec((1, nblk, SUBLANE, LANE), lambda b, p, i: (b, 0, 0, 0))],
        out_specs=qspec,
        scratch_shapes=[pltpu.VMEM((2, blk, LANE), F32)] * 3,
        compiler_params=_cparams(("parallel", "parallel", "parallel")),
        name="fox_prompt",
    )(fq, qa, fkb, ka, fvb, base)


def _log_gamma(h):
    return math.log(1.0 - 2.0 ** (-5.0 - h))


def _group_norm_gate(o, w, rg):
    mu = jnp.mean(o, axis=-1, keepdims=True)
    var = jnp.mean(jnp.square(o - mu), axis=-1, keepdims=True)
    on = (o - mu) * lax.rsqrt(var + EPS)
    rg = rg.astype(F32)
    return (rg * jax.nn.sigmoid(rg)) * (on * w)


def _ret_prompt_body(q_ref, k_ref, v_ref, g_ref, w_ref, o_ref, st_ref, s_scr):
    t = pl.program_id(1)

    @pl.when(t == 0)
    def _():
        s_scr[...] = jnp.zeros_like(s_scr)

    L = RET_CHUNK
    ri = lax.broadcasted_iota(jnp.int32, (L, L), 0).astype(F32)
    ci = lax.broadcasted_iota(jnp.int32, (L, L), 1).astype(F32)
    diff = ri - ci
    rv = lax.broadcasted_iota(jnp.int32, (L, DV_R), 0).astype(F32)
    for c in range(q_ref.shape[1] // L):
        rows = slice(c * L, (c + 1) * L)
        for h in range(H_R):
            lg = _log_gamma(h)
            kc = slice(h * DK_R, (h + 1) * DK_R)
            vc = slice(h * DV_R, (h + 1) * DV_R)
            q = q_ref[0, rows, kc]
            k = k_ref[0, rows, kc]
            v = v_ref[0, rows, vc]
            decay = jnp.where(diff >= 0, jnp.exp(lg * jnp.maximum(diff, 0.0)), 0.0)
            scores = _dot_nt(q, k) * decay
            s0 = s_scr[h]
            o = _dot(scores.astype(BF16), v) + _dot(q, s0.astype(BF16)) * jnp.exp(lg * (rv + 1.0))
            kd = (k.astype(F32) * jnp.exp(lg * (L - 1.0 - ri))).astype(BF16)
            s_scr[h] = s0 * math.exp(lg * L) + _dot_tn(kd, v)
            o_ref[0, rows, vc] = _group_norm_gate(o, w_ref[:, vc], g_ref[0, rows, vc]).astype(o_ref.dtype)

    @pl.when(t == pl.num_programs(1) - 1)
    def _():
        st_ref[0] = s_scr[...]


def _ret_prompt(rq, rk, rv, rg, norm_w):
    nb, nt, _ = rq.shape
    tb = min(RET_BLOCK, nt)
    spec = lambda w: pl.BlockSpec((1, tb, w), lambda b, t: (b, t, 0))
    return pl.pallas_call(
        _ret_prompt_body,
        out_shape=[jax.ShapeDtypeStruct((nb, nt, D_RV), BF16),
                   jax.ShapeDtypeStruct((nb, H_R, DK_R, DV_R), F32)],
        grid=(nb, nt // tb),
        in_specs=[spec(D_RQK), spec(D_RQK), spec(D_RV), spec(D_RV), _const_spec((1, D_RV))],
        out_specs=[spec(D_RV), pl.BlockSpec((1, H_R, DK_R, DV_R), lambda b, t: (b, 0, 0, 0))],
        scratch_shapes=[pltpu.VMEM((H_R, DK_R, DV_R), F32)],
        compiler_params=_cparams(("parallel", "arbitrary")),
        name="ret_prompt",
    )(rq, rk, rv, rg, norm_w)


def _ret_decode_body(q_ref, k_ref, v_ref, g_ref, w_ref, s_ref, o_ref, so_ref):
    rr = lax.broadcasted_iota(jnp.int32, (DK_R, DK_R), 0)
    cc = lax.broadcasted_iota(jnp.int32, (DK_R, DK_R), 1)
    eye = rr == cc
    for n in range(q_ref.shape[0]):
        for h in range(H_R):
            gamma = math.exp(_log_gamma(h))
            kc = slice(h * DK_R, (h + 1) * DK_R)
            vc = slice(h * DV_R, (h + 1) * DV_R)
            q = q_ref[n, :, kc]
            k = k_ref[n, :, kc]
            v = v_ref[n, :, vc]
            s0 = s_ref[n, h]
            qk = jnp.sum(q.astype(F32) * k.astype(F32), axis=-1, keepdims=True)
            q_s = _dot(jnp.broadcast_to(q, (2 * SUBLANE, DK_R)), s0.astype(BF16))[:1]
            o = qk.astype(BF16).astype(F32) * v.astype(F32) + q_s * gamma
            k_diag = jnp.where(eye, jnp.broadcast_to(k.astype(F32), (DK_R, DK_R)), 0.0).astype(BF16)
            v_rows = jnp.broadcast_to(v.astype(F32), (DK_R, DV_R)).astype(BF16)
            so_ref[n, h] = s0 * gamma + _dot(k_diag, v_rows)
            o_ref[n, :, vc] = _group_norm_gate(o, w_ref[:, vc], g_ref[n, :, vc]).astype(o_ref.dtype)


def _ret_decode(rq, rk, rv, rg, norm_w, state):
    ns = rq.shape[0]
    sb = min(DEC_SEQS, ns)
    spec = lambda w: pl.BlockSpec((sb, 1, w), lambda i: (i, 0, 0))
    st_spec = pl.BlockSpec((sb, H_R, DK_R, DV_R), lambda i: (i, 0, 0, 0))
    return pl.pallas_call(
        _ret_decode_body,
        out_shape=[jax.ShapeDtypeStruct((ns, 1, D_RV), BF16), jax.ShapeDtypeStruct(state.shape, F32)],
        grid=(ns // sb,),
        in_specs=[spec(D_RQK), spec(D_RQK), spec(D_RV), spec(D_RV), _const_spec((1, D_RV)), st_spec],
        out_specs=[spec(D_RV), st_spec],
        compiler_params=_cparams(("parallel",)),
        name="ret_decode",
    )(rq, rk, rv, rg, norm_w, state)


def _fox_decode_body(pt_ref, q_ref, kn_ref, vn_ref, lfn_ref, us_ref, e_ref, *rest, n_pages):
    k_refs = rest[:n_pages]
    v_refs = rest[n_pages:2 * n_pages]
    lf_refs = rest[2 * n_pages:3 * n_pages]
    o_ref = rest[3 * n_pages]
    s_scr, lfpad, m_run, l_run, tail, acc_run, qm_scr = rest[3 * n_pages + 1:]
    g = pl.program_id(1)
    e = e_ref[...]

    @pl.when(g == 0)
    def _():
        q = q_ref[0].astype(F32)
        row = lax.broadcasted_iota(jnp.int32, (LANE, D_FQK), 0)
        lane = lax.broadcasted_iota(jnp.int32, (LANE, D_FQK), 1)
        qm = jnp.where(lane // DH_F == row, jnp.broadcast_to(q, (LANE, D_FQK)), 0.0).astype(BF16)
        qm_scr[...] = qm
        kn = jnp.broadcast_to(kn_ref[0].astype(BF16), (2 * SUBLANE, D_FQK))
        m_run[...] = _dot_nt(kn, qm)[:1]
        l_run[...] = jnp.ones_like(l_run)
        first = lax.broadcasted_iota(jnp.int32, acc_run.shape, 0) == 0
        acc_run[...] = jnp.where(first, jnp.broadcast_to(vn_ref[0], acc_run.shape), 0.0)
        tail[...] = lfn_ref[0]
        lfpad[...] = jnp.zeros_like(lfpad)

    qm = qm_scr[...]
    us = us_ref[...]
    m_step = jnp.full((1, LANE), NEG_INF, F32)
    for r in range(n_pages):
        s = _dot_nt(k_refs[r][0].astype(BF16), qm)
        lfpad[:, 0:H_F] = lf_refs[r][0]
        x = lfpad[...]
        suf = _dot3_rhs(us, x)
        t = tail[...]
        s = s + (t + suf)
        tail[...] = t + suf[:1] + x[:1]
        s_scr[r] = s
        m_step = jnp.maximum(m_step, jnp.max(s, axis=0, keepdims=True))

    m_prev = m_run[...]
    m_new = jnp.maximum(m_prev, m_step)
    alpha = jnp.exp(m_prev - m_new)
    l_step = jnp.zeros((1, LANE), F32)
    accv = jnp.zeros(acc_run.shape, F32)
    for r in range(n_pages):
        p = jnp.exp(s_scr[r] - m_new)
        l_step = l_step + jnp.sum(p, axis=0, keepdims=True)
        pv = _dot(p.astype(BF16), e) * v_refs[r][0]
        accv = accv + jnp.sum(pv.reshape(PAGE_SIZE // SUBLANE, SUBLANE, D_FQK), axis=0)
    alpha_e = _dot3_lhs(jnp.broadcast_to(alpha, (SUBLANE, LANE)), e)
    acc_run[...] = alpha_e * acc_run[...] + accv
    l_run[...] = alpha * l_run[...] + l_step
    m_run[...] = m_new

    @pl.when(g == pl.num_programs(1) - 1)
    def _():
        l_e = _dot3_lhs(jnp.broadcast_to(l_run[...], (SUBLANE, LANE)), e)[:1]
        o_ref[0] = (jnp.sum(acc_run[...], axis=0, keepdims=True) / l_e).astype(o_ref.dtype)


def _fox_decode(page_table, fq, fk, fv, lfp, cache_k, cache_v, cache_lf):
    ns, n_tab = page_table.shape
    pg = min(DEC_PAGES, n_tab)
    n_phys = cache_k.shape[0]
    ck = cache_k.reshape(n_phys, PAGE_SIZE, D_FQK)
    cv = cache_v.reshape(n_phys, PAGE_SIZE, D_FQK)
    r = np.arange(LANE)
    us = jnp.asarray((r[None, :] > r[:, None]).astype(np.float32), dtype=BF16)
    hd = np.arange(D_FQK) // DH_F
    e = jnp.asarray((hd[None, :] == r[:, None]).astype(np.float32), dtype=BF16)
    row = lambda w: pl.BlockSpec((1, 1, w), lambda s, g, pt: (s, 0, 0))

    def page_spec(w, idx):
        return pl.BlockSpec((1, PAGE_SIZE, w), lambda s, g, pt: (pt[s, n_tab - 1 - (g * pg + idx)], 0, 0))

    grid_spec = pltpu.PrefetchScalarGridSpec(
        num_scalar_prefetch=1,
        grid=(ns, n_tab // pg),
        in_specs=[row(D_FQK), row(D_FQK), row(D_FQK), row(LANE), _const_spec((LANE, LANE)),
                  _const_spec((LANE, D_FQK)),
                  *[page_spec(D_FQK, i) for i in range(pg)], *[page_spec(D_FQK, i) for i in range(pg)],
                  *[page_spec(H_F, i) for i in range(pg)]],
        out_specs=row(D_FQK),
        scratch_shapes=[pltpu.VMEM((pg, PAGE_SIZE, LANE), F32), pltpu.VMEM((PAGE_SIZE, LANE), F32),
                        pltpu.VMEM((1, LANE), F32), pltpu.VMEM((1, LANE), F32), pltpu.VMEM((1, LANE), F32),
                        pltpu.VMEM((SUBLANE, D_FQK), F32), pltpu.VMEM((LANE, D_FQK), BF16)],
    )
    return pl.pallas_call(
        functools.partial(_fox_decode_body, n_pages=pg),
        out_shape=jax.ShapeDtypeStruct((ns, 1, D_FQK), BF16),
        grid_spec=grid_spec,
        compiler_params=_cparams(("parallel", "arbitrary")),
        name="fox_decode",
    )(page_table, fq, fk, fv, lfp, us, e, *([ck] * pg), *([cv] * pg), *([cache_lf] * pg))


def _outproj_body(x_ref, gt_ref, or_ref, of_ref, gr_ref, gf_ref, wb_ref, wo_ref, o_ref, *, per_row):
    gt = _mod_rows(gt_ref, per_row)
    merged = (gr_ref[0].astype(F32) * _dot(or_ref[0], wb_ref[:D_RV, :])
              + gf_ref[0].astype(F32) * _dot(of_ref[0], wb_ref[D_RV:, :]))
    y = _dot(merged.astype(BF16), wo_ref[...])
    o_ref[0] = x_ref[0] + gt * y


def _outproj(x, mod, per_row, row_off, o_r, o_f, gr, gf, w_branch, w_out):
    nb, nt, _ = x.shape
    tm = min(ROW_TILE, nt)
    row = lambda w: pl.BlockSpec((1, tm, w), lambda b, t: (b, t, 0))
    return pl.pallas_call(
        functools.partial(_outproj_body, per_row=per_row),
        out_shape=jax.ShapeDtypeStruct(x.shape, F32),
        grid=(nb, nt // tm),
        in_specs=[row(D_MODEL), _mod_specs(1, per_row, nt, row_off)[2], row(D_RV), row(D_FQK),
                  row(D_MODEL), row(D_MODEL), _const_spec((D_RV + D_FQK, D_MODEL)),
                  _const_spec((D_MODEL, D_MODEL))],
        out_specs=row(D_MODEL),
        compiler_params=_cparams(("parallel", "parallel")),
        name="outproj",
    )(x, mod, o_r, o_f, gr, gf, w_branch, w_out)


def _rope_tables(pos):
    half = DK_R // 2
    inv = ROPE_BASE ** (-jnp.arange(half, dtype=F32) / half)
    ang = pos.astype(F32)[:, None] * inv[None, :]
    cos, sin = jnp.cos(ang), jnp.sin(ang)
    return jnp.concatenate([cos, cos], axis=1), jnp.concatenate([-sin, sin], axis=1)


def _permute_w_in(w_in):
    sizes = (D_RQK, D_RQK, D_RV, D_RV, D_FQK, D_FQK, D_FQK, H_F, D_MODEL, D_MODEL)
    offs = [0]
    for s in sizes:
        offs.append(offs[-1] + s)
    part = lambda i: w_in[:, offs[i]:offs[i + 1]]
    fl = jnp.pad(part(7), ((0, 0), (0, LANE - H_F)))
    return jnp.concatenate([part(i) for i in (0, 1, 2, 3, 4, 5, 6, 8, 9)] + [fl], axis=1).astype(BF16)


def kernel(x_prompt, x_sample, cache_k, cache_v, cache_logf, state_ret, page_table, c_prompt, c_sample,
           w_ada, b_ada, w_ffa_up, w_ffa_down, w_in, b_forget, q_norm_w, k_norm_w, ret_norm_w,
           w_branch, w_out, w_ffb_up, w_ffb_down):
    nb, nt, _ = x_prompt.shape
    ns = x_sample.shape[0]
    depth = w_ada.shape[0]
    past_len = page_table.shape[1] * PAGE_SIZE
    assert x_sample.shape[1] == 1 and nb <= SUBLANE and ns % SUBLANE == 0 and nt % ATT_BLK == 0

    pad = (-(ns + SUBLANE)) % (2 * SUBLANE)
    c_all = jnp.concatenate([c_sample, c_prompt, jnp.zeros((SUBLANE - nb + pad, D_MODEL), F32)], axis=0)
    cos_p, sin_p = _rope_tables(jnp.arange(nt, dtype=jnp.int32))
    cos_s, sin_s = _rope_tables(jnp.full((ns,), past_len, jnp.int32))
    hd = np.arange(D_FQK) // DH_F
    bd = jnp.asarray((hd[:, None] == hd[None, :]).astype(np.float32) / DH_F, dtype=BF16)

    y_p = x_prompt
    y_s = x_sample.reshape(1, ns, D_MODEL)
    outs = [[] for _ in range(8)]
    for l in range(depth):
        bf = lambda w: w[l].astype(BF16)
        w_up_a, w_dn_a, w_up_b, w_dn_b = bf(w_ffa_up), bf(w_ffa_down), bf(w_ffb_up), bf(w_ffb_down)
        w_br, w_o = bf(w_branch), bf(w_out)
        w_inp = _permute_w_in(w_in[l])
        bf_row = jnp.pad(b_forget[l], (0, LANE - H_F)).reshape(1, LANE)
        qn = jnp.tile(q_norm_w[l], H_F).reshape(1, D_FQK)
        kn = jnp.tile(k_norm_w[l], H_F).reshape(1, D_FQK)
        rn = ret_norm_w[l].reshape(1, D_RV)
        mod = _adaln(c_all, bf(w_ada), b_ada[l])

        y_p = _ffn(y_p, mod, 0, False, ns, w_up_a, w_dn_a)
        pj = _inproj(y_p, mod, False, ns, cos_p, sin_p, w_inp, bf_row, qn, kn, bd)
        qa, ka, base = _forget_terms(pj["lfp"], ATT_BLK)
        o_f = _fox_prompt(pj["fq"], qa, pj["fkb"], ka, pj["fvb"], base, ATT_BLK)
        o_r, st_p = _ret_prompt(pj["rq"], pj["rk"], pj["rv"], pj["rg"], rn)
        y_p = _outproj(y_p, mod, False, ns, o_r, o_f, pj["gr"], pj["gf"], w_br, w_o)
        y_p = _ffn(y_p, mod, 2, False, ns, w_up_b, w_dn_b)

        y_s = _ffn(y_s, mod, 0, True, 0, w_up_a, w_dn_a)
        sj = _inproj(y_s, mod, True, 0, cos_s, sin_s, w_inp, bf_row, qn, kn, bd)
        seq = lambda a: a.reshape(ns, 1, a.shape[-1])
        o_f_s = _fox_decode(page_table, seq(sj["fq"]), seq(sj["fk"]), seq(sj["fv"]), seq(sj["lfp"]),
                            cache_k[l], cache_v[l], cache_logf[l])
        o_r_s, st_s = _ret_decode(seq(sj["rq"]), seq(sj["rk"]), seq(sj["rv"]), seq(sj["rg"]), rn, state_ret[l])
        y_s = _outproj(y_s, mod, True, 0, o_r_s.reshape(1, ns, D_RV), o_f_s.reshape(1, ns, D_FQK),
                       sj["gr"], sj["gf"], w_br, w_o)
        y_s = _ffn(y_s, mod, 2, True, 0, w_up_b, w_dn_b)

        for dst, val in zip(outs, (pj["fk"].reshape(nb, nt, H_F, DH_F), pj["fv"].reshape(nb, nt, H_F, DH_F),
                                   pj["lfc"], st_p, sj["fk"].reshape(ns, 1, H_F, DH_F),
                                   sj["fv"].reshape(ns, 1, H_F, DH_F), sj["lfc"].reshape(ns, 1, H_F), st_s)):
            dst.append(val)

    return (y_p, y_s.reshape(ns, 1, D_MODEL), *[jnp.stack(o) for o in outs])
```

```python
import functools
import math

import jax
import jax.numpy as jnp
import numpy as np
from jax import lax
from jax.experimental import pallas as pl
from jax.experimental.pallas import tpu as pltpu

D_MODEL = 1024
PAGE_SIZE = 128
H_R, DK_R, DV_R = 4, 128, 256
RET_CHUNK = 128
ROPE_BASE = 10000.0
H_F, DH_F = 8, 64
D_FF = 2816
EPS = 1e-6
N_SUB = 3
NEG_INF = -1e30
EXP_UNDERFLOW = 110.0

D_RQK = H_R * DK_R
D_RV = H_R * DV_R
D_FQK = H_F * DH_F
N_PAIR = H_F // 2
LANE = 128
SUBLANE = 8

F32 = jnp.float32
BF16 = jnp.bfloat16

_SEG = {}
_off = 0
for _name, _w in (("rq", D_RQK), ("rk", D_RQK), ("rv", D_RV), ("rg", D_RV), ("fq", D_FQK), ("fk", D_FQK),
                  ("fv", D_FQK), ("gr", D_MODEL), ("gf", D_MODEL), ("fl", LANE)):
    _SEG[_name] = (_off, _off + _w)
    _off += _w
D_IN_PAD = _off

ROW_TILE = 512
ATT_BLK = 512
FF_CHUNK = D_FF // 2
RET_BLOCK = 512
DEC_PAGES = 16
DEC_SEQS = 4
VMEM_LIMIT = 56 * 2 ** 20


def _cparams(sem):
    return pltpu.CompilerParams(dimension_semantics=sem, vmem_limit_bytes=VMEM_LIMIT)


def _const_spec(shape):
    nd = len(shape)
    return pl.BlockSpec(shape, lambda *_: (0,) * nd, pipeline_mode=pl.Buffered(1))


def _dot(a, b):
    return jnp.dot(a, b, preferred_element_type=F32)


def _dot_nt(a, b):
    return lax.dot_general(a, b, (((1,), (1,)), ((), ())), preferred_element_type=F32)


def _dot_tn(a, b):
    return lax.dot_general(a, b, (((0,), (0,)), ((), ())), preferred_element_type=F32)


def _split3(x):
    hi = x.astype(BF16)
    r1 = x - hi.astype(F32)
    mid = r1.astype(BF16)
    lo = (r1 - mid.astype(F32)).astype(BF16)
    return hi, mid, lo


def _dot3_rhs(a_bf16, x):
    hi, mid, lo = _split3(x)
    return _dot(a_bf16, hi) + _dot(a_bf16, mid) + _dot(a_bf16, lo)


def _dot3_lhs(x, b_bf16):
    hi, mid, lo = _split3(x)
    return _dot(hi, b_bf16) + _dot(mid, b_bf16) + _dot(lo, b_bf16)


def _log_sigmoid(x):
    return jnp.minimum(x, 0.0) - jnp.log1p(jnp.exp(-jnp.abs(x)))


def _mod_rows(ref, per_row):
    if per_row:
        return ref[0]
    return ref[0, pl.ds(pl.program_id(0), 1), :]


def _modulated_norm(x, shift, scale):
    ms = jnp.mean(x * x, axis=-1, keepdims=True)
    return (x * lax.rsqrt(ms + EPS)) * (1.0 + scale) + shift


def _mod_specs(sub, per_row, n_rows, row_off):
    specs = []
    for k in range(3):
        j = sub * 3 + k
        if per_row:
            specs.append(pl.BlockSpec((1, n_rows, D_MODEL), lambda b, t, j=j: (j, 0, 0)))
        else:
            specs.append(pl.BlockSpec((1, SUBLANE, D_MODEL), lambda b, t, j=j: (j, row_off // SUBLANE, 0)))
    return specs


def _adaln_body(c_ref, w_ref, b_ref, o_ref):
    c = c_ref[...]
    s = c * jax.nn.sigmoid(c)
    o_ref[0] = _dot(s.astype(BF16), w_ref[...]) + b_ref[...]


def _adaln(c_all, w_ada, b_ada):
    rows = c_all.shape[0]
    n = N_SUB * 3
    return pl.pallas_call(
        _adaln_body,
        out_shape=jax.ShapeDtypeStruct((n, rows, D_MODEL), F32),
        grid=(n,),
        in_specs=[pl.BlockSpec((rows, D_MODEL), lambda j: (0, 0)),
                  pl.BlockSpec((D_MODEL, D_MODEL), lambda j: (0, j)),
                  pl.BlockSpec((1, D_MODEL), lambda j: (0, j))],
        out_specs=pl.BlockSpec((1, rows, D_MODEL), lambda j: (j, 0, 0)),
        compiler_params=_cparams(("parallel",)),
        name="adaln",
    )(c_all, w_ada, b_ada.reshape(1, -1))


def _ffn_body(x_ref, sh_ref, sc_ref, gt_ref, wup_ref, wdn_ref, o_ref, *, per_row):
    x = x_ref[0]
    sh, sc, gt = (_mod_rows(r, per_row) for r in (sh_ref, sc_ref, gt_ref))
    h = _modulated_norm(x, sh, sc).astype(BF16)
    acc = None
    for c in range(D_FF // FF_CHUNK):
        lo = c * FF_CHUNK
        g = _dot(h, wup_ref[:, lo:lo + FF_CHUNK])
        u = _dot(h, wup_ref[:, D_FF + lo:D_FF + lo + FF_CHUNK])
        a = (g * jax.nn.sigmoid(g) * u).astype(BF16)
        part = _dot(a, wdn_ref[lo:lo + FF_CHUNK, :])
        acc = part if acc is None else acc + part
    o_ref[0] = x + (0.5 * gt) * acc


def _ffn(x, mod, sub, per_row, row_off, w_up, w_dn):
    nb, nt, _ = x.shape
    tm = min(ROW_TILE, nt)
    x_spec = pl.BlockSpec((1, tm, D_MODEL), lambda b, t: (b, t, 0))
    return pl.pallas_call(
        functools.partial(_ffn_body, per_row=per_row),
        out_shape=jax.ShapeDtypeStruct(x.shape, F32),
        grid=(nb, nt // tm),
        in_specs=[x_spec, *_mod_specs(sub, per_row, nt, row_off),
                  _const_spec((D_MODEL, 2 * D_FF)), _const_spec((D_FF, D_MODEL))],
        out_specs=x_spec,
        compiler_params=_cparams(("parallel", "parallel")),
        name="ffn",
    )(x, mod, mod, mod, w_up, w_dn)


def _inproj_body(x_ref, sh_ref, sc_ref, cos_ref, sin_ref, w_ref, bf_ref, qn_ref, kn_ref, bd_ref,
                 rq_o, rk_o, rv_o, rg_o, fq_o, fkb_o, fvb_o, fkt_o, fvt_o, gr_o, gf_o, lft_o, lfp_o, *, per_row):
    x = x_ref[0]
    sh, sc = (_mod_rows(r, per_row) for r in (sh_ref, sc_ref))
    h = _modulated_norm(x, sh, sc).astype(BF16)

    def seg(name):
        lo, hi = _SEG[name]
        return _dot(h, w_ref[:, lo:hi])

    cos, sin = cos_ref[...], sin_ref[...]

    def rope_store(z, out, scale):
        for hd in range(H_R):
            zh = z[:, hd * DK_R:(hd + 1) * DK_R]
            r = zh * cos + pltpu.roll(zh, DK_R // 2, 1) * sin
            if scale != 1.0:
                r = r * scale
            out[0, :, hd * DK_R:(hd + 1) * DK_R] = r.astype(out.dtype)

    rope_store(seg("rq"), rq_o, 1.0)
    rope_store(seg("rk"), rk_o, DK_R ** -0.5)
    rv_o[0] = seg("rv").astype(BF16)
    rg_o[0] = seg("rg").astype(BF16)

    bd = bd_ref[...]

    def head_rms(z, w):
        z2 = z * z
        hi = z2.astype(BF16)
        lo = (z2 - hi.astype(F32)).astype(BF16)
        ms = _dot(hi, bd) + _dot(lo, bd)
        return (z * lax.rsqrt(ms + EPS)) * w

    fq_o[0] = (head_rms(seg("fq"), qn_ref[...]) * (DH_F ** -0.5)).astype(BF16)
    fk = head_rms(seg("fk"), kn_ref[...])
    fkt_o[0] = fk.T
    fkb_o[0] = fk.astype(BF16)
    fv = seg("fv")
    fvt_o[0] = fv.T
    fvb_o[0] = fv.astype(BF16)
    gr_o[0] = jax.nn.sigmoid(seg("gr")).astype(BF16)
    gf_o[0] = jax.nn.sigmoid(seg("gf")).astype(BF16)

    logf = _log_sigmoid(seg("fl") + bf_ref[...])
    lane = lax.broadcasted_iota(jnp.int32, logf.shape, 1)
    logf = jnp.where(lane < H_F, logf, 0.0)
    lfp_o[0] = logf
    lft_o[0] = logf.T[:H_F]


def _inproj(x, mod, per_row, row_off, cos_t, sin_t, w_in, bf_row, qn, kn, bd):
    nb, nt, _ = x.shape
    tm = min(ROW_TILE, nt)
    row = lambda w: pl.BlockSpec((1, tm, w), lambda b, t: (b, t, 0))
    col = lambda w: pl.BlockSpec((1, w, tm), lambda b, t: (b, 0, t))
    outs = [("rq", D_RQK, BF16), ("rk", D_RQK, BF16), ("rv", D_RV, BF16), ("rg", D_RV, BF16),
            ("fq", D_FQK, BF16), ("fkb", D_FQK, BF16), ("fvb", D_FQK, BF16), ("fkt", D_FQK, None),
            ("fvt", D_FQK, None), ("gr", D_MODEL, BF16), ("gf", D_MODEL, BF16), ("lft", H_F, None),
            ("lfp", LANE, F32)]
    shp = lambda w, dt: jax.ShapeDtypeStruct((nb, w, nt) if dt is None else (nb, nt, w), dt or F32)
    res = pl.pallas_call(
        functools.partial(_inproj_body, per_row=per_row),
        out_shape=[shp(w, dt) for _, w, dt in outs],
        grid=(nb, nt // tm),
        in_specs=[row(D_MODEL), *_mod_specs(1, per_row, nt, row_off)[:2],
                  pl.BlockSpec((tm, DK_R), lambda b, t: (t, 0)), pl.BlockSpec((tm, DK_R), lambda b, t: (t, 0)),
                  _const_spec((D_MODEL, D_IN_PAD)), _const_spec((1, LANE)), _const_spec((1, D_FQK)),
                  _const_spec((1, D_FQK)), _const_spec((D_FQK, D_FQK))],
        out_specs=[col(w) if dt is None else row(w) for _, w, dt in outs],
        compiler_params=_cparams(("parallel", "parallel")),
        name="inproj",
    )(x, mod, mod, cos_t, sin_t, w_in, bf_row, qn, kn, bd)
    return {name: r for (name, _, _), r in zip(outs, res)}


def _forget_terms_body(lf_ref, ltri_ref, ones_ref, pq_ref, pk_ref, qrow_ref, krow_ref, thr_ref,
                       qa_o, ka_o, base_o, skip_o, carry, bases):
    i = pl.program_id(1)

    @pl.when(i == 0)
    def _():
        carry[...] = jnp.zeros_like(carry)
        bases[...] = jnp.zeros_like(bases)

    ltri = ltri_ref[...]
    blk = lf_ref.shape[1]
    run = jnp.zeros((1, LANE), F32)
    for r in range(blk // LANE):
        rows = slice(r * LANE, (r + 1) * LANE)
        c = _dot3_rhs(ltri, lf_ref[0, rows, :]) + run
        run = c[LANE - 1:LANE, :]
        hi, mid, lo = _split3(c)
        qa = _dot(hi, pq_ref[0]) + _dot(mid, pq_ref[1]) + _dot(lo, pq_ref[2]) + qrow_ref[...]
        ka = _dot(hi, pk_ref[0]) + _dot(mid, pk_ref[1]) + _dot(lo, pk_ref[2]) + krow_ref[...]
        qa_o[0, rows, :] = qa.astype(BF16)
        ka_o[0, rows, :] = ka.astype(BF16)

    base = carry[...]
    rr = lax.broadcasted_iota(jnp.int32, (2 * SUBLANE, LANE), 0)
    cc = lax.broadcasted_iota(jnp.int32, (2 * SUBLANE, LANE), 1)
    diag = jnp.where(rr == cc, jnp.broadcast_to(base, (2 * SUBLANE, LANE)), 0.0)
    base_o[0, 0] = _dot3_lhs(diag, ones_ref[...])[:SUBLANE]
    carry[...] = base + run

    bases[pl.ds(i, 1), :] = base
    jj = lax.broadcasted_iota(jnp.int32, bases.shape, 0)
    far = (jj >= 1) & (jj <= i) & (bases[...] - base > thr_ref[...])
    skip_o[0, 0] = jnp.broadcast_to(jnp.sum(jnp.where(far, 1.0, 0.0), axis=0, keepdims=True), (SUBLANE, LANE))


def _forget_tables():
    w = N_PAIR * LANE
    pq, pk = np.zeros((3, LANE, w), np.float32), np.zeros((3, LANE, w), np.float32)
    qrow, krow = np.zeros((1, w), np.float32), np.zeros((1, w), np.float32)
    for h in range(H_F):
        l0 = (h // 2) * LANE + (h % 2) * DH_F
        for s in range(3):
            pq[s, h, l0 + s] = 1.0
            pk[s, h, l0 + 3 + s] = -1.0
            qrow[0, l0 + 3 + s] = 1.0
            krow[0, l0 + s] = 1.0
    r = np.arange(LANE)
    ltri = (r[None, :] <= r[:, None]).astype(np.float32)
    as_bf16 = lambda a: jnp.asarray(a, dtype=BF16)
    return (as_bf16(ltri), jnp.ones((LANE, LANE), BF16), as_bf16(pq), as_bf16(pk), jnp.asarray(qrow),
            jnp.asarray(krow))


def _forget_terms(lfp, blk, thr):
    nb, nt, _ = lfp.shape
    nblk = nt // blk
    w = N_PAIR * LANE
    ltri, ones, pq, pk, qrow, krow = _forget_tables()
    stat = jax.ShapeDtypeStruct((nb, nblk, SUBLANE, LANE), F32)
    stat_spec = pl.BlockSpec((1, 1, SUBLANE, LANE), lambda b, i: (b, i, 0, 0))
    return pl.pallas_call(
        _forget_terms_body,
        out_shape=[jax.ShapeDtypeStruct((nb, nt, w), BF16), jax.ShapeDtypeStruct((nb, nt, w), BF16), stat, stat],
        grid=(nb, nblk),
        in_specs=[pl.BlockSpec((1, blk, LANE), lambda b, i: (b, i, 0)),
                  _const_spec((LANE, LANE)), _const_spec((LANE, LANE)), _const_spec((3, LANE, w)),
                  _const_spec((3, LANE, w)), _const_spec((1, w)), _const_spec((1, w)), _const_spec((1, LANE))],
        out_specs=[pl.BlockSpec((1, blk, w), lambda b, i: (b, i, 0)),
                   pl.BlockSpec((1, blk, w), lambda b, i: (b, i, 0)), stat_spec, stat_spec],
        scratch_shapes=[pltpu.VMEM((1, LANE), F32), pltpu.VMEM((-(-nblk // SUBLANE) * SUBLANE, LANE), F32)],
        compiler_params=_cparams(("parallel", "arbitrary")),
        name="forget_terms",
    )(lfp, ltri, ones, pq, pk, qrow, krow, jnp.broadcast_to(thr.astype(F32), (1, LANE)))


def _fox_prompt_body(skip_ref, q_ref, qa_ref, k_ref, ka_ref, v_ref, base_ref, o_ref, m_scr, l_scr, acc_scr):
    b = pl.program_id(0)
    hp = pl.program_id(1)
    i = pl.program_id(2)
    tq = q_ref.shape[1]
    tk = tq
    lane2 = lax.broadcasted_iota(jnp.int32, (1, 2 * LANE), 1)
    first = (lane2 % LANE) < DH_F
    qf = jnp.concatenate([q_ref[0], qa_ref[0]], axis=1)
    zero = jnp.zeros_like(qf)
    q_heads = (jnp.where(first, qf, zero), jnp.where(first, zero, qf))

    m_scr[...] = jnp.full(m_scr.shape, NEG_INF, F32)
    l_scr[...] = jnp.zeros_like(l_scr)
    acc_scr[...] = jnp.zeros_like(acc_scr)

    def step(hh, j, masked):
        off = pl.multiple_of(j * tk, tk)
        kf = jnp.concatenate([k_ref[0, pl.ds(off, tk), :], ka_ref[0, pl.ds(off, tk), :]], axis=1)
        s = _dot_nt(q_heads[hh], kf)
        if masked:
            row = lax.broadcasted_iota(jnp.int32, (tq, tk), 0)
            col = lax.broadcasted_iota(jnp.int32, (tq, tk), 1)
            s = jnp.where(col <= row, s, NEG_INF)
        h = 2 * hp + hh
        d = base_ref[0, i, pl.ds(h, 1), :] - base_ref[0, j, pl.ds(h, 1), :]
        m_prev = m_scr[hh]
        m_next = jnp.maximum(m_prev, jnp.max(s, axis=1, keepdims=True) + d)
        alpha = jnp.exp(m_prev - m_next)
        sub = m_next - d
        p = jnp.exp(s - jnp.concatenate([sub] * (tk // LANE), axis=1))
        l_scr[hh] = alpha * l_scr[hh] + jnp.sum(p, axis=1, keepdims=True)
        acc_scr[hh] = alpha * acc_scr[hh] + _dot(p.astype(BF16), v_ref[0, pl.ds(off, tk), :])
        m_scr[hh] = m_next

    for hh in range(2):
        j0 = skip_ref[(b * pl.num_programs(2) + i) * H_F + 2 * hp + hh]

        def full_step(j, carry, hh=hh):
            step(hh, j, False)
            return carry

        lax.fori_loop(j0, i, full_step, 0)
        step(hh, i, True)

    lane = lax.broadcasted_iota(jnp.int32, (1, LANE), 1)
    o = jnp.where(lane < DH_F, acc_scr[0] / l_scr[0], acc_scr[1] / l_scr[1])
    o_ref[0] = o.astype(o_ref.dtype)


def _fox_prompt(skip, fq, qa, fkb, ka, fvb, base, blk):
    nb, nt, _ = fq.shape
    nblk = nt // blk
    qspec = pl.BlockSpec((1, blk, LANE), lambda b, p, i, sk: (b, i, p))
    kspec = pl.BlockSpec((1, nt, LANE), lambda b, p, i, sk: (b, 0, p))
    grid_spec = pltpu.PrefetchScalarGridSpec(
        num_scalar_prefetch=1,
        grid=(nb, N_PAIR, nblk),
        in_specs=[qspec, qspec, kspec, kspec, kspec,
                  pl.BlockSpec((1, nblk, SUBLANE, LANE), lambda b, p, i, sk: (b, 0, 0, 0))],
        out_specs=qspec,
        scratch_shapes=[pltpu.VMEM((2, blk, LANE), F32)] * 3,
    )
    return pl.pallas_call(
        _fox_prompt_body,
        out_shape=jax.ShapeDtypeStruct((nb, nt, D_FQK), BF16),
        grid_spec=grid_spec,
        compiler_params=_cparams(("parallel", "parallel", "parallel")),
        name="fox_prompt",
    )(skip, fq, qa, fkb, ka, fvb, base)


def _log_gamma(h):
    return math.log(1.0 - 2.0 ** (-5.0 - h))


def _group_norm_gate(o, w, rg):
    mu = jnp.mean(o, axis=-1, keepdims=True)
    var = jnp.mean(jnp.square(o - mu), axis=-1, keepdims=True)
    on = (o - mu) * lax.rsqrt(var + EPS)
    rg = rg.astype(F32)
    return (rg * jax.nn.sigmoid(rg)) * (on * w)


def _ret_prompt_body(q_ref, k_ref, v_ref, g_ref, w_ref, o_ref, st_ref, s_scr):
    t = pl.program_id(1)

    @pl.when(t == 0)
    def _():
        s_scr[...] = jnp.zeros_like(s_scr)

    L = RET_CHUNK
    ri = lax.broadcasted_iota(jnp.int32, (L, L), 0).astype(F32)
    ci = lax.broadcasted_iota(jnp.int32, (L, L), 1).astype(F32)
    diff = ri - ci
    rv = lax.broadcasted_iota(jnp.int32, (L, DV_R), 0).astype(F32)
    for c in range(q_ref.shape[1] // L):
        rows = slice(c * L, (c + 1) * L)
        for h in range(H_R):
            lg = _log_gamma(h)
            kc = slice(h * DK_R, (h + 1) * DK_R)
            vc = slice(h * DV_R, (h + 1) * DV_R)
            q = q_ref[0, rows, kc]
            k = k_ref[0, rows, kc]
            v = v_ref[0, rows, vc]
            decay = jnp.where(diff >= 0, jnp.exp(lg * jnp.maximum(diff, 0.0)), 0.0)
            scores = _dot_nt(q, k) * decay
            s0 = s_scr[h]
            o = _dot(scores.astype(BF16), v) + _dot(q, s0.astype(BF16)) * jnp.exp(lg * (rv + 1.0))
            kd = (k.astype(F32) * jnp.exp(lg * (L - 1.0 - ri))).astype(BF16)
            s_scr[h] = s0 * math.exp(lg * L) + _dot_tn(kd, v)
            o_ref[0, rows, vc] = _group_norm_gate(o, w_ref[:, vc], g_ref[0, rows, vc]).astype(o_ref.dtype)

    @pl.when(t == pl.num_programs(1) - 1)
    def _():
        st_ref[0] = s_scr[...]


def _ret_prompt(rq, rk, rv, rg, norm_w):
    nb, nt, _ = rq.shape
    tb = min(RET_BLOCK, nt)
    spec = lambda w: pl.BlockSpec((1, tb, w), lambda b, t: (b, t, 0))
    return pl.pallas_call(
        _ret_prompt_body,
        out_shape=[jax.ShapeDtypeStruct((nb, nt, D_RV), BF16),
                   jax.ShapeDtypeStruct((nb, H_R, DK_R, DV_R), F32)],
        grid=(nb, nt // tb),
        in_specs=[spec(D_RQK), spec(D_RQK), spec(D_RV), spec(D_RV), _const_spec((1, D_RV))],
        out_specs=[spec(D_RV), pl.BlockSpec((1, H_R, DK_R, DV_R), lambda b, t: (b, 0, 0, 0))],
        scratch_shapes=[pltpu.VMEM((H_R, DK_R, DV_R), F32)],
        compiler_params=_cparams(("parallel", "arbitrary")),
        name="ret_prompt",
    )(rq, rk, rv, rg, norm_w)


def _ret_decode_body(q_ref, k_ref, v_ref, g_ref, w_ref, s_ref, o_ref, so_ref):
    rr = lax.broadcasted_iota(jnp.int32, (DK_R, DK_R), 0)
    cc = lax.broadcasted_iota(jnp.int32, (DK_R, DK_R), 1)
    eye = rr == cc
    for n in range(q_ref.shape[0]):
        for h in range(H_R):
            gamma = math.exp(_log_gamma(h))
            kc = slice(h * DK_R, (h + 1) * DK_R)
            vc = slice(h * DV_R, (h + 1) * DV_R)
            q = q_ref[n, :, kc]
            k = k_ref[n, :, kc]
            v = v_ref[n, :, vc]
            s0 = s_ref[n, h]
            qk = jnp.sum(q.astype(F32) * k.astype(F32), axis=-1, keepdims=True)
            q_s = _dot(jnp.broadcast_to(q, (2 * SUBLANE, DK_R)), s0.astype(BF16))[:1]
            o = qk.astype(BF16).astype(F32) * v.astype(F32) + q_s * gamma
            k_diag = jnp.where(eye, jnp.broadcast_to(k.astype(F32), (DK_R, DK_R)), 0.0).astype(BF16)
            v_rows = jnp.broadcast_to(v.astype(F32), (DK_R, DV_R)).astype(BF16)
            so_ref[n, h] = s0 * gamma + _dot(k_diag, v_rows)
            o_ref[n, :, vc] = _group_norm_gate(o, w_ref[:, vc], g_ref[n, :, vc]).astype(o_ref.dtype)


def _ret_decode(rq, rk, rv, rg, norm_w, state):
    ns = rq.shape[0]
    sb = min(DEC_SEQS, ns)
    spec = lambda w: pl.BlockSpec((sb, 1, w), lambda i: (i, 0, 0))
    st_spec = pl.BlockSpec((sb, H_R, DK_R, DV_R), lambda i: (i, 0, 0, 0))
    return pl.pallas_call(
        _ret_decode_body,
        out_shape=[jax.ShapeDtypeStruct((ns, 1, D_RV), BF16), jax.ShapeDtypeStruct(state.shape, F32)],
        grid=(ns // sb,),
        in_specs=[spec(D_RQK), spec(D_RQK), spec(D_RV), spec(D_RV), _const_spec((1, D_RV)), st_spec],
        out_specs=[spec(D_RV), st_spec],
        compiler_params=_cparams(("parallel",)),
        name="ret_decode",
    )(rq, rk, rv, rg, norm_w, state)


def _page_totals_body(lf_ref, o_ref):
    o_ref[...] = jnp.sum(lf_ref[...], axis=-1, keepdims=True)


def _page_totals(clf_t):
    n_phys = clf_t.shape[0]
    pb = max(d for d in range(1, min(n_phys, 1024) + 1) if n_phys % d == 0)
    return pl.pallas_call(
        _page_totals_body,
        out_shape=jax.ShapeDtypeStruct((n_phys, H_F, 1), F32),
        grid=(n_phys // pb,),
        in_specs=[pl.BlockSpec((pb, H_F, PAGE_SIZE), lambda i: (i, 0, 0))],
        out_specs=pl.BlockSpec((pb, H_F, 1), lambda i: (i, 0, 0)),
        compiler_params=_cparams(("parallel",)),
        name="page_totals",
    )(clf_t)


def _fox_decode_body(pt_ref, need_ref, q_ref, kn_ref, vn_ref, lfn_ref, msuf_ref, ones_ref, pex_ref, et_ref,
                     *rest, n_pages):
    k_refs = rest[:n_pages]
    v_refs = rest[n_pages:2 * n_pages]
    lf_refs = rest[2 * n_pages:3 * n_pages]
    o_ref = rest[3 * n_pages]
    m_run, l_run, tail, s_self, acc_run, qbd = rest[3 * n_pages + 1:]
    g = pl.program_id(1)
    need = need_ref[pl.program_id(0)]
    et = et_ref[...]
    stat = (SUBLANE, LANE)
    zpad = jnp.zeros((LANE - SUBLANE, LANE), F32)

    @pl.when(g == 0)
    def _():
        q = q_ref[0].astype(F32)
        row = lax.broadcasted_iota(jnp.int32, (2 * SUBLANE, D_FQK), 0)
        lane = lax.broadcasted_iota(jnp.int32, (2 * SUBLANE, D_FQK), 1)
        qf = jnp.where(lane // DH_F == row, jnp.broadcast_to(q, (2 * SUBLANE, D_FQK)), 0.0)
        qbd[...] = qf.astype(BF16)
        ss = jnp.sum(qf * kn_ref[0].astype(F32), axis=1, keepdims=True)[:SUBLANE]
        s_self[...] = jnp.broadcast_to(ss, stat)
        m_run[...] = jnp.broadcast_to(ss, stat)
        l_run[...] = jnp.zeros(stat, F32)
        acc_run[...] = jnp.zeros_like(acc_run)
        tail[...] = jnp.broadcast_to(lfn_ref[0], stat)

    @pl.when(g * n_pages < need)
    def _():
        kt = jnp.concatenate([k_refs[r][0].astype(BF16) for r in range(n_pages)], axis=1)
        s_all = _dot(qbd[...], kt)
        rows = [lf_refs[r][0] for r in range(n_pages)]
        if n_pages * SUBLANE < LANE:
            rows.append(jnp.zeros((LANE - n_pages * SUBLANE, LANE), F32))
        lf = jnp.concatenate(rows, axis=0)
        suf = _dot3_lhs(lf, msuf_ref[...])
        tot = _dot3_lhs(lf, ones_ref[...])
        pre = _dot3_rhs(pex_ref[...], tot)
        t_in = tail[...]
        last = slice((n_pages - 1) * SUBLANE, n_pages * SUBLANE)
        tail[...] = t_in + pre[last] + tot[last]
        s_pages = []
        m_step = jnp.full((SUBLANE, 1), NEG_INF, F32)
        for r in range(n_pages):
            pr = slice(r * SUBLANE, (r + 1) * SUBLANE)
            sr = s_all[:SUBLANE, r * LANE:(r + 1) * LANE] + (suf[pr] + pre[pr] + t_in)
            sr = jnp.where(g * n_pages + r < need, sr, NEG_INF)
            s_pages.append(sr)
            m_step = jnp.maximum(m_step, jnp.max(sr, axis=1, keepdims=True))
        m_prev = m_run[...]
        m_new = jnp.maximum(m_prev, m_step)
        alpha = jnp.exp(m_prev - m_new)
        l_step = jnp.zeros((SUBLANE, 1), F32)
        p_pages = []
        for sr in s_pages:
            p = jnp.exp(sr - m_new)
            l_step = l_step + jnp.sum(p, axis=1, keepdims=True)
            p_pages.append(jnp.concatenate([p, jnp.zeros(stat, F32)], axis=0))
        p_all = jnp.concatenate(p_pages, axis=1).astype(BF16)
        vt = jnp.concatenate([v_refs[r][0].astype(BF16) for r in range(n_pages)], axis=1)
        a_rows = _dot3_rhs(et, jnp.concatenate([alpha, zpad], axis=0))
        acc_run[...] = acc_run[...] * a_rows[:, :2 * SUBLANE] + _dot_nt(vt, p_all)
        l_run[...] = alpha * l_run[...] + l_step
        m_run[...] = m_new

    @pl.when(g == pl.num_programs(1) - 1)
    def _():
        w_self = jnp.exp(s_self[...] - m_run[...])
        l_rows = _dot3_rhs(et, jnp.concatenate([l_run[...] + w_self, zpad], axis=0))
        w_rows = _dot3_rhs(et, jnp.concatenate([w_self, zpad], axis=0))
        rr = lax.broadcasted_iota(jnp.int32, acc_run.shape, 0)
        cc = lax.broadcasted_iota(jnp.int32, acc_run.shape, 1)
        acc_col = jnp.sum(jnp.where(rr // DH_F == cc, acc_run[...], 0.0), axis=1, keepdims=True)
        o_ref[0] = (acc_col + w_rows[:, :1] * vn_ref[0]) / l_rows[:, :1]


def _fox_decode(page_table, fq, fk, fv_col, lfn_col, ck_t, cv_t, clf_t, thr):
    ns, n_tab = page_table.shape
    pg = min(DEC_PAGES, n_tab)
    assert pg * SUBLANE <= LANE and n_tab % pg == 0

    tot = _page_totals(clf_t).reshape(-1, H_F)[page_table]
    after = lfn_col.reshape(ns, 1, H_F) + jnp.cumsum(tot[:, ::-1], axis=1)[:, ::-1] - tot
    need = jnp.maximum(jnp.sum(jnp.any(after >= -thr, axis=-1), axis=1), 1).astype(jnp.int32)

    r = np.arange(LANE)
    msuf = jnp.asarray((r[:, None] > r[None, :]).astype(np.float32), dtype=BF16)
    same_head = (r[:, None] % SUBLANE) == (r[None, :] % SUBLANE)
    in_step = (r[:, None] < pg * SUBLANE) & (r[None, :] < pg * SUBLANE)
    pex = jnp.asarray((same_head & in_step & (r[None, :] // SUBLANE < r[:, None] // SUBLANE)).astype(np.float32),
                      dtype=BF16)
    et = jnp.asarray(((np.arange(D_FQK) // DH_F)[:, None] == r[None, :]).astype(np.float32), dtype=BF16)
    row = lambda w: pl.BlockSpec((1, 1, w), lambda s, g, pt, nd: (s, 0, 0))
    col = lambda w: pl.BlockSpec((1, w, 1), lambda s, g, pt, nd: (s, 0, 0))

    def page_spec(rows, idx):
        def index_map(s, g, pt, nd):
            pos = g * pg + idx
            n = nd[s]
            last = idx + pg * ((n - 1 - idx) // pg)
            eff = jnp.where(pos < n, pos, jnp.where(idx < n, last, idx))
            return (pt[s, n_tab - 1 - eff], 0, 0)
        return pl.BlockSpec((1, rows, PAGE_SIZE), index_map)

    stat = pltpu.VMEM((SUBLANE, LANE), F32)
    grid_spec = pltpu.PrefetchScalarGridSpec(
        num_scalar_prefetch=2,
        grid=(ns, n_tab // pg),
        in_specs=[row(D_FQK), row(D_FQK), col(D_FQK), col(H_F), _const_spec((LANE, LANE)),
                  _const_spec((LANE, LANE)), _const_spec((LANE, LANE)), _const_spec((D_FQK, LANE)),
                  *[page_spec(D_FQK, i) for i in range(pg)], *[page_spec(D_FQK, i) for i in range(pg)],
                  *[page_spec(H_F, i) for i in range(pg)]],
        out_specs=col(D_FQK),
        scratch_shapes=[stat, stat, stat, stat, pltpu.VMEM((D_FQK, 2 * SUBLANE), F32),
                        pltpu.VMEM((2 * SUBLANE, D_FQK), BF16)],
    )
    return pl.pallas_call(
        functools.partial(_fox_decode_body, n_pages=pg),
        out_shape=jax.ShapeDtypeStruct((ns, D_FQK, 1), F32),
        grid_spec=grid_spec,
        compiler_params=_cparams(("parallel", "arbitrary")),
        name="fox_decode",
    )(page_table, need, fq, fk, fv_col, lfn_col, msuf, jnp.ones((LANE, LANE), BF16), pex, et,
      *([ck_t] * pg), *([cv_t] * pg), *([clf_t] * pg))


def _outproj_body(x_ref, gt_ref, or_ref, of_ref, gr_ref, gf_ref, wb_ref, wo_ref, o_ref, *, per_row):
    gt = _mod_rows(gt_ref, per_row)
    merged = (gr_ref[0].astype(F32) * _dot(or_ref[0], wb_ref[:D_RV, :])
              + gf_ref[0].astype(F32) * _dot(of_ref[0], wb_ref[D_RV:, :]))
    y = _dot(merged.astype(BF16), wo_ref[...])
    o_ref[0] = x_ref[0] + gt * y


def _outproj(x, mod, per_row, row_off, o_r, o_f, gr, gf, w_branch, w_out):
    nb, nt, _ = x.shape
    tm = min(ROW_TILE, nt)
    row = lambda w: pl.BlockSpec((1, tm, w), lambda b, t: (b, t, 0))
    return pl.pallas_call(
        functools.partial(_outproj_body, per_row=per_row),
        out_shape=jax.ShapeDtypeStruct(x.shape, F32),
        grid=(nb, nt // tm),
        in_specs=[row(D_MODEL), _mod_specs(1, per_row, nt, row_off)[2], row(D_RV), row(D_FQK),
                  row(D_MODEL), row(D_MODEL), _const_spec((D_RV + D_FQK, D_MODEL)),
                  _const_spec((D_MODEL, D_MODEL))],
        out_specs=row(D_MODEL),
        compiler_params=_cparams(("parallel", "parallel")),
        name="outproj",
    )(x, mod, o_r, o_f, gr, gf, w_branch, w_out)


def _rope_tables(pos):
    half = DK_R // 2
    inv = ROPE_BASE ** (-jnp.arange(half, dtype=F32) / half)
    ang = pos.astype(F32)[:, None] * inv[None, :]
    cos, sin = jnp.cos(ang), jnp.sin(ang)
    return jnp.concatenate([cos, cos], axis=1), jnp.concatenate([-sin, sin], axis=1)


def _permute_w_in(w_in):
    sizes = (D_RQK, D_RQK, D_RV, D_RV, D_FQK, D_FQK, D_FQK, H_F, D_MODEL, D_MODEL)
    offs = [0]
    for s in sizes:
        offs.append(offs[-1] + s)
    part = lambda i: w_in[:, offs[i]:offs[i + 1]]
    fl = jnp.pad(part(7), ((0, 0), (0, LANE - H_F)))
    return jnp.concatenate([part(i) for i in (0, 1, 2, 3, 4, 5, 6, 8, 9)] + [fl], axis=1).astype(BF16)


def kernel(x_prompt, x_sample, cache_k, cache_v, cache_logf, state_ret, page_table, c_prompt, c_sample,
           w_ada, b_ada, w_ffa_up, w_ffa_down, w_in, b_forget, q_norm_w, k_norm_w, ret_norm_w,
           w_branch, w_out, w_ffb_up, w_ffb_down):
    nb, nt, _ = x_prompt.shape
    ns = x_sample.shape[0]
    depth = w_ada.shape[0]
    past_len = page_table.shape[1] * PAGE_SIZE
    assert x_sample.shape[1] == 1 and nb <= SUBLANE and ns % SUBLANE == 0 and nt % ATT_BLK == 0

    pad = (-(ns + SUBLANE)) % (2 * SUBLANE)
    c_all = jnp.concatenate([c_sample, c_prompt, jnp.zeros((SUBLANE - nb + pad, D_MODEL), F32)], axis=0)
    cos_p, sin_p = _rope_tables(jnp.arange(nt, dtype=jnp.int32))
    cos_s, sin_s = _rope_tables(jnp.full((ns,), past_len, jnp.int32))
    hd = np.arange(D_FQK) // DH_F
    bd = jnp.asarray((hd[:, None] == hd[None, :]).astype(np.float32) / DH_F, dtype=BF16)

    y_p = x_prompt
    y_s = x_sample.reshape(1, ns, D_MODEL)
    outs = [[] for _ in range(8)]
    for l in range(depth):
        bf = lambda w: w[l].astype(BF16)
        w_up_a, w_dn_a, w_up_b, w_dn_b = bf(w_ffa_up), bf(w_ffa_down), bf(w_ffb_up), bf(w_ffb_down)
        w_br, w_o = bf(w_branch), bf(w_out)
        w_inp = _permute_w_in(w_in[l])
        bf_row = jnp.pad(b_forget[l], (0, LANE - H_F)).reshape(1, LANE)
        qn = jnp.tile(q_norm_w[l], H_F).reshape(1, D_FQK)
        kn = jnp.tile(k_norm_w[l], H_F).reshape(1, D_FQK)
        rn = ret_norm_w[l].reshape(1, D_RV)
        mod = _adaln(c_all, bf(w_ada), b_ada[l])
        qk_bound = 1.02 * DH_F ** 0.5 * jnp.max(jnp.abs(q_norm_w[l])) * jnp.max(jnp.abs(k_norm_w[l]))
        thr = EXP_UNDERFLOW + 2.0 * qk_bound
        page_view = lambda c, w: jnp.transpose(c, (0, 2, 3, 1)).reshape(c.shape[0], w, PAGE_SIZE)

        y_p = _ffn(y_p, mod, 0, False, ns, w_up_a, w_dn_a)
        pj = _inproj(y_p, mod, False, ns, cos_p, sin_p, w_inp, bf_row, qn, kn, bd)
        qa, ka, base, skip = _forget_terms(pj["lfp"], ATT_BLK, thr)
        skip = skip[:, :, 0, :H_F].astype(jnp.int32).reshape(-1)
        o_f = _fox_prompt(skip, pj["fq"], qa, pj["fkb"], ka, pj["fvb"], base, ATT_BLK)
        o_r, st_p = _ret_prompt(pj["rq"], pj["rk"], pj["rv"], pj["rg"], rn)
        y_p = _outproj(y_p, mod, False, ns, o_r, o_f, pj["gr"], pj["gf"], w_br, w_o)
        y_p = _ffn(y_p, mod, 2, False, ns, w_up_b, w_dn_b)

        y_s = _ffn(y_s, mod, 0, True, 0, w_up_a, w_dn_a)
        sj = _inproj(y_s, mod, True, 0, cos_s, sin_s, w_inp, bf_row, qn, kn, bd)
        seq = lambda a: a.reshape(ns, 1, a.shape[-1])
        col = lambda a: jnp.transpose(a, (2, 1, 0))
        o_f_s = _fox_decode(page_table, seq(sj["fq"]), seq(sj["fkb"]), col(sj["fvt"]), col(sj["lft"]),
                            page_view(cache_k[l], D_FQK), page_view(cache_v[l], D_FQK),
                            jnp.transpose(cache_logf[l], (0, 2, 1)), thr)
        o_r_s, st_s = _ret_decode(seq(sj["rq"]), seq(sj["rk"]), seq(sj["rv"]), seq(sj["rg"]), rn, state_ret[l])
        y_s = _outproj(y_s, mod, True, 0, o_r_s.reshape(1, ns, D_RV), o_f_s.reshape(1, ns, D_FQK).astype(BF16),
                       sj["gr"], sj["gf"], w_br, w_o)
        y_s = _ffn(y_s, mod, 2, True, 0, w_up_b, w_dn_b)

        rows = lambda a: jnp.transpose(a.reshape(a.shape[0], H_F, DH_F, a.shape[2]), (0, 3, 1, 2))
        for dst, val in zip(outs, (rows(pj["fkt"]), rows(pj["fvt"]), jnp.transpose(pj["lft"], (0, 2, 1)), st_p,
                                   rows(sj["fkt"]).reshape(ns, 1, H_F, DH_F),
                                   rows(sj["fvt"]).reshape(ns, 1, H_F, DH_F),
                                   jnp.transpose(sj["lft"], (0, 2, 1)).reshape(ns, 1, H_F), st_s)):
            dst.append(val)

    return (y_p, y_s.reshape(ns, 1, D_MODEL), *[jnp.stack(o) for o in outs])
```

```python
import functools
import math

import jax
import jax.numpy as jnp
import numpy as np
from jax import lax
from jax.experimental import pallas as pl
from jax.experimental.pallas import tpu as pltpu

D_MODEL = 1024
PAGE_SIZE = 128
H_R, DK_R, DV_R = 4, 128, 256
RET_CHUNK = 128
ROPE_BASE = 10000.0
H_F, DH_F = 8, 64
D_FF = 2816
EPS = 1e-6
N_SUB = 3
NEG_INF = -1e30
EXP_UNDERFLOW = 110.0

D_RQK = H_R * DK_R
D_RV = H_R * DV_R
D_FQK = H_F * DH_F
N_PAIR = H_F // 2
LANE = 128
SUBLANE = 8

F32 = jnp.float32
BF16 = jnp.bfloat16

_SEG = {}
_off = 0
for _name, _w in (("rq", D_RQK), ("rk", D_RQK), ("rv", D_RV), ("rg", D_RV), ("fq", D_FQK), ("fk", D_FQK),
                  ("fv", D_FQK), ("gr", D_MODEL), ("gf", D_MODEL), ("fl", LANE)):
    _SEG[_name] = (_off, _off + _w)
    _off += _w
D_IN_PAD = _off

ROW_TILE = 512
ATT_BLK = 512
FFN_TILE = 1024
FF_CHUNK = 256
RET_BLOCK = 512
DEC_PAGES = 16
DEC_SEQS = 4
VMEM_LIMIT = 56 * 2 ** 20


def _cparams(sem):
    return pltpu.CompilerParams(dimension_semantics=sem, vmem_limit_bytes=VMEM_LIMIT)


def _const_spec(shape):
    nd = len(shape)
    return pl.BlockSpec(shape, lambda *_: (0,) * nd, pipeline_mode=pl.Buffered(1))


def _dot(a, b):
    return jnp.dot(a, b, preferred_element_type=F32)


def _dot_nt(a, b):
    return lax.dot_general(a, b, (((1,), (1,)), ((), ())), preferred_element_type=F32)


def _dot_tn(a, b):
    return lax.dot_general(a, b, (((0,), (0,)), ((), ())), preferred_element_type=F32)


def _split3(x):
    hi = x.astype(BF16)
    r1 = x - hi.astype(F32)
    mid = r1.astype(BF16)
    lo = (r1 - mid.astype(F32)).astype(BF16)
    return hi, mid, lo


def _dot3_rhs(a_bf16, x):
    hi, mid, lo = _split3(x)
    return _dot(a_bf16, hi) + _dot(a_bf16, mid) + _dot(a_bf16, lo)


def _dot3_lhs(x, b_bf16):
    hi, mid, lo = _split3(x)
    return _dot(hi, b_bf16) + _dot(mid, b_bf16) + _dot(lo, b_bf16)


def _log_sigmoid(x):
    return jnp.minimum(x, 0.0) - jnp.log1p(jnp.exp(-jnp.abs(x)))


def _mod_rows(ref, per_row):
    if per_row:
        return ref[0]
    return ref[0, pl.ds(pl.program_id(0), 1), :]


def _modulated_norm(x, shift, scale):
    ms = jnp.mean(x * x, axis=-1, keepdims=True)
    return (x * lax.rsqrt(ms + EPS)) * (1.0 + scale) + shift


def _mod_specs(sub, per_row, n_rows, row_off):
    specs = []
    for k in range(3):
        j = sub * 3 + k
        if per_row:
            specs.append(pl.BlockSpec((1, n_rows, D_MODEL), lambda b, t, j=j: (j, 0, 0)))
        else:
            specs.append(pl.BlockSpec((1, SUBLANE, D_MODEL), lambda b, t, j=j: (j, row_off // SUBLANE, 0)))
    return specs


def _adaln_body(c_ref, w_ref, b_ref, o_ref):
    c = c_ref[...]
    s = c * jax.nn.sigmoid(c)
    o_ref[0] = _dot(s.astype(BF16), w_ref[...]) + b_ref[...]


def _adaln(c_all, w_ada, b_ada):
    rows = c_all.shape[0]
    n = N_SUB * 3
    return pl.pallas_call(
        _adaln_body,
        out_shape=jax.ShapeDtypeStruct((n, rows, D_MODEL), F32),
        grid=(n,),
        in_specs=[pl.BlockSpec((rows, D_MODEL), lambda j: (0, 0)),
                  pl.BlockSpec((D_MODEL, D_MODEL), lambda j: (0, j)),
                  pl.BlockSpec((1, D_MODEL), lambda j: (0, j))],
        out_specs=pl.BlockSpec((1, rows, D_MODEL), lambda j: (j, 0, 0)),
        compiler_params=_cparams(("parallel",)),
        name="adaln",
    )(c_all, w_ada, b_ada.reshape(1, -1))


def _ffn_body(x_ref, sh_ref, sc_ref, gt_ref, wup_ref, wdn_ref, o_ref, *, per_row):
    x = x_ref[0]
    sh, sc, gt = (_mod_rows(r, per_row) for r in (sh_ref, sc_ref, gt_ref))
    h = _modulated_norm(x, sh, sc).astype(BF16)
    acc = None
    for c in range(D_FF // FF_CHUNK):
        lo = c * FF_CHUNK
        g = _dot(h, wup_ref[:, lo:lo + FF_CHUNK])
        u = _dot(h, wup_ref[:, D_FF + lo:D_FF + lo + FF_CHUNK])
        a = (g * jax.nn.sigmoid(g) * u).astype(BF16)
        part = _dot(a, wdn_ref[lo:lo + FF_CHUNK, :])
        acc = part if acc is None else acc + part
    o_ref[0] = x + (0.5 * gt) * acc


def _ffn(x, mod, sub, per_row, row_off, w_up, w_dn):
    nb, nt, _ = x.shape
    tm = min(FFN_TILE, nt)
    x_spec = pl.BlockSpec((1, tm, D_MODEL), lambda b, t: (b, t, 0))
    return pl.pallas_call(
        functools.partial(_ffn_body, per_row=per_row),
        out_shape=jax.ShapeDtypeStruct(x.shape, F32),
        grid=(nb, nt // tm),
        in_specs=[x_spec, *_mod_specs(sub, per_row, nt, row_off),
                  _const_spec((D_MODEL, 2 * D_FF)), _const_spec((D_FF, D_MODEL))],
        out_specs=x_spec,
        compiler_params=_cparams(("parallel", "parallel")),
        name="ffn",
    )(x, mod, mod, mod, w_up, w_dn)


def _inproj_body(x_ref, sh_ref, sc_ref, cos_ref, sin_ref, w_ref, bf_ref, qn_ref, kn_ref, bd_ref,
                 rq_o, rk_o, rv_o, rg_o, fq_o, fkb_o, fvb_o, fkt_o, fvt_o, gr_o, gf_o, lft_o, lfp_o, *, per_row):
    x = x_ref[0]
    sh, sc = (_mod_rows(r, per_row) for r in (sh_ref, sc_ref))
    h = _modulated_norm(x, sh, sc).astype(BF16)

    def seg(name):
        lo, hi = _SEG[name]
        return _dot(h, w_ref[:, lo:hi])

    cos, sin = cos_ref[...], sin_ref[...]

    def rope_store(z, out, scale):
        for hd in range(H_R):
            zh = z[:, hd * DK_R:(hd + 1) * DK_R]
            r = zh * cos + pltpu.roll(zh, DK_R // 2, 1) * sin
            if scale != 1.0:
                r = r * scale
            out[0, :, hd * DK_R:(hd + 1) * DK_R] = r.astype(out.dtype)

    rope_store(seg("rq"), rq_o, 1.0)
    rope_store(seg("rk"), rk_o, DK_R ** -0.5)
    rv_o[0] = seg("rv").astype(BF16)
    rg_o[0] = seg("rg").astype(BF16)

    bd = bd_ref[...]

    def head_rms(z, w):
        z2 = z * z
        hi = z2.astype(BF16)
        lo = (z2 - hi.astype(F32)).astype(BF16)
        ms = _dot(hi, bd) + _dot(lo, bd)
        return (z * lax.rsqrt(ms + EPS)) * w

    fq_o[0] = (head_rms(seg("fq"), qn_ref[...]) * (DH_F ** -0.5)).astype(BF16)
    fk = head_rms(seg("fk"), kn_ref[...])
    fkt_o[0] = fk.T
    fkb_o[0] = fk.astype(BF16)
    fv = seg("fv")
    fvt_o[0] = fv.T
    fvb_o[0] = fv.astype(BF16)
    gr_o[0] = jax.nn.sigmoid(seg("gr")).astype(BF16)
    gf_o[0] = jax.nn.sigmoid(seg("gf")).astype(BF16)

    logf = _log_sigmoid(seg("fl") + bf_ref[...])
    lane = lax.broadcasted_iota(jnp.int32, logf.shape, 1)
    logf = jnp.where(lane < H_F, logf, 0.0)
    lfp_o[0] = logf
    lft_o[0] = logf.T[:H_F]


def _inproj(x, mod, per_row, row_off, cos_t, sin_t, w_in, bf_row, qn, kn, bd):
    nb, nt, _ = x.shape
    tm = min(ROW_TILE, nt)
    row = lambda w: pl.BlockSpec((1, tm, w), lambda b, t: (b, t, 0))
    col = lambda w: pl.BlockSpec((1, w, tm), lambda b, t: (b, 0, t))
    outs = [("rq", D_RQK, BF16), ("rk", D_RQK, BF16), ("rv", D_RV, BF16), ("rg", D_RV, BF16),
            ("fq", D_FQK, BF16), ("fkb", D_FQK, BF16), ("fvb", D_FQK, BF16), ("fkt", D_FQK, None),
            ("fvt", D_FQK, None), ("gr", D_MODEL, BF16), ("gf", D_MODEL, BF16), ("lft", H_F, None),
            ("lfp", LANE, F32)]
    shp = lambda w, dt: jax.ShapeDtypeStruct((nb, w, nt) if dt is None else (nb, nt, w), dt or F32)
    res = pl.pallas_call(
        functools.partial(_inproj_body, per_row=per_row),
        out_shape=[shp(w, dt) for _, w, dt in outs],
        grid=(nb, nt // tm),
        in_specs=[row(D_MODEL), *_mod_specs(1, per_row, nt, row_off)[:2],
                  pl.BlockSpec((tm, DK_R), lambda b, t: (t, 0)), pl.BlockSpec((tm, DK_R), lambda b, t: (t, 0)),
                  _const_spec((D_MODEL, D_IN_PAD)), _const_spec((1, LANE)), _const_spec((1, D_FQK)),
                  _const_spec((1, D_FQK)), _const_spec((D_FQK, D_FQK))],
        out_specs=[col(w) if dt is None else row(w) for _, w, dt in outs],
        compiler_params=_cparams(("parallel", "parallel")),
        name="inproj",
    )(x, mod, mod, cos_t, sin_t, w_in, bf_row, qn, kn, bd)
    return {name: r for (name, _, _), r in zip(outs, res)}


def _forget_terms_body(lf_ref, ltri_ref, ones_ref, pq_ref, pk_ref, qrow_ref, krow_ref, thr_ref,
                       qa_o, ka_o, base_o, skip_o, carry, bases):
    i = pl.program_id(1)

    @pl.when(i == 0)
    def _():
        carry[...] = jnp.zeros_like(carry)
        bases[...] = jnp.zeros_like(bases)

    blk = lf_ref.shape[1]
    c = _dot3_rhs(ltri_ref[...], lf_ref[0])
    run = c[blk - 1:blk, :]
    hi, mid, lo = _split3(c)
    qa = _dot(hi, pq_ref[0]) + _dot(mid, pq_ref[1]) + _dot(lo, pq_ref[2]) + qrow_ref[...]
    ka = _dot(hi, pk_ref[0]) + _dot(mid, pk_ref[1]) + _dot(lo, pk_ref[2]) + krow_ref[...]
    qa_o[0] = qa.astype(BF16)
    ka_o[0] = ka.astype(BF16)

    base = carry[...]
    rr = lax.broadcasted_iota(jnp.int32, (2 * SUBLANE, LANE), 0)
    cc = lax.broadcasted_iota(jnp.int32, (2 * SUBLANE, LANE), 1)
    diag = jnp.where(rr == cc, jnp.broadcast_to(base, (2 * SUBLANE, LANE)), 0.0)
    base_o[0, 0] = _dot3_lhs(diag, ones_ref[...])[:SUBLANE]
    carry[...] = base + run

    bases[pl.ds(i, 1), :] = base
    jj = lax.broadcasted_iota(jnp.int32, bases.shape, 0)
    far = (jj >= 1) & (jj <= i) & (bases[...] - base > thr_ref[...])
    skip_o[0, 0] = jnp.broadcast_to(jnp.sum(jnp.where(far, 1.0, 0.0), axis=0, keepdims=True), (SUBLANE, LANE))


def _forget_tables(blk):
    w = N_PAIR * LANE
    pq, pk = np.zeros((3, LANE, w), np.float32), np.zeros((3, LANE, w), np.float32)
    qrow, krow = np.zeros((1, w), np.float32), np.zeros((1, w), np.float32)
    for h in range(H_F):
        l0 = (h // 2) * LANE + (h % 2) * DH_F
        for s in range(3):
            pq[s, h, l0 + s] = 1.0
            pk[s, h, l0 + 3 + s] = -1.0
            qrow[0, l0 + 3 + s] = 1.0
            krow[0, l0 + s] = 1.0
    r = np.arange(blk)
    ltri = (r[None, :] <= r[:, None]).astype(np.float32)
    as_bf16 = lambda a: jnp.asarray(a, dtype=BF16)
    return (as_bf16(ltri), jnp.ones((LANE, LANE), BF16), as_bf16(pq), as_bf16(pk), jnp.asarray(qrow),
            jnp.asarray(krow))


def _forget_terms(lfp, blk, thr):
    nb, nt, _ = lfp.shape
    nblk = nt // blk
    w = N_PAIR * LANE
    ltri, ones, pq, pk, qrow, krow = _forget_tables(blk)
    stat = jax.ShapeDtypeStruct((nb, nblk, SUBLANE, LANE), F32)
    stat_spec = pl.BlockSpec((1, 1, SUBLANE, LANE), lambda b, i: (b, i, 0, 0))
    return pl.pallas_call(
        _forget_terms_body,
        out_shape=[jax.ShapeDtypeStruct((nb, nt, w), BF16), jax.ShapeDtypeStruct((nb, nt, w), BF16), stat, stat],
        grid=(nb, nblk),
        in_specs=[pl.BlockSpec((1, blk, LANE), lambda b, i: (b, i, 0)),
                  _const_spec((blk, blk)), _const_spec((LANE, LANE)), _const_spec((3, LANE, w)),
                  _const_spec((3, LANE, w)), _const_spec((1, w)), _const_spec((1, w)), _const_spec((1, LANE))],
        out_specs=[pl.BlockSpec((1, blk, w), lambda b, i: (b, i, 0)),
                   pl.BlockSpec((1, blk, w), lambda b, i: (b, i, 0)), stat_spec, stat_spec],
        scratch_shapes=[pltpu.VMEM((1, LANE), F32), pltpu.VMEM((-(-nblk // SUBLANE) * SUBLANE, LANE), F32)],
        compiler_params=_cparams(("parallel", "arbitrary")),
        name="forget_terms",
    )(lfp, ltri, ones, pq, pk, qrow, krow, jnp.broadcast_to(thr.astype(F32), (1, LANE)))


def _fox_prompt_body(skip_ref, q_ref, qa_ref, k_ref, ka_ref, v_ref, base_ref, o_ref, m_scr, l_scr, acc_scr):
    b = pl.program_id(0)
    hp = pl.program_id(1)
    i = pl.program_id(2)
    tq = q_ref.shape[1]
    tk = tq
    lane2 = lax.broadcasted_iota(jnp.int32, (1, 2 * LANE), 1)
    first = (lane2 % LANE) < DH_F
    qf = jnp.concatenate([q_ref[0], qa_ref[0]], axis=1)
    zero = jnp.zeros_like(qf)
    q_heads = (jnp.where(first, qf, zero), jnp.where(first, zero, qf))

    m_scr[...] = jnp.full(m_scr.shape, NEG_INF, F32)
    l_scr[...] = jnp.zeros_like(l_scr)
    acc_scr[...] = jnp.zeros_like(acc_scr)

    def step(hh, j, masked):
        off = pl.multiple_of(j * tk, tk)
        kf = jnp.concatenate([k_ref[0, pl.ds(off, tk), :], ka_ref[0, pl.ds(off, tk), :]], axis=1)
        s = _dot_nt(q_heads[hh], kf)
        if masked:
            row = lax.broadcasted_iota(jnp.int32, (tq, tk), 0)
            col = lax.broadcasted_iota(jnp.int32, (tq, tk), 1)
            s = jnp.where(col <= row, s, NEG_INF)
        h = 2 * hp + hh
        d = base_ref[0, i, pl.ds(h, 1), :] - base_ref[0, j, pl.ds(h, 1), :]
        m_prev = m_scr[hh]
        m_next = jnp.maximum(m_prev, jnp.max(s, axis=1, keepdims=True) + d)
        alpha = jnp.exp(m_prev - m_next)
        sub = m_next - d
        p = jnp.exp(s - jnp.concatenate([sub] * (tk // LANE), axis=1))
        l_scr[hh] = alpha * l_scr[hh] + jnp.sum(p, axis=1, keepdims=True)
        acc_scr[hh] = alpha * acc_scr[hh] + _dot(p.astype(BF16), v_ref[0, pl.ds(off, tk), :])
        m_scr[hh] = m_next

    for hh in range(2):
        j0 = skip_ref[(b * pl.num_programs(2) + i) * H_F + 2 * hp + hh]

        def full_step(j, carry, hh=hh):
            step(hh, j, False)
            return carry

        lax.fori_loop(j0, i, full_step, 0)
        step(hh, i, True)

    lane = lax.broadcasted_iota(jnp.int32, (1, LANE), 1)
    o = jnp.where(lane < DH_F, acc_scr[0] / l_scr[0], acc_scr[1] / l_scr[1])
    o_ref[0] = o.astype(o_ref.dtype)


def _fox_prompt(skip, fq, qa, fkb, ka, fvb, base, blk):
    nb, nt, _ = fq.shape
    nblk = nt // blk
    qspec = pl.BlockSpec((1, blk, LANE), lambda b, p, i, sk: (b, i, p))
    kspec = pl.BlockSpec((1, nt, LANE), lambda b, p, i, sk: (b, 0, p))
    grid_spec = pltpu.PrefetchScalarGridSpec(
        num_scalar_prefetch=1,
        grid=(nb, N_PAIR, nblk),
        in_specs=[qspec, qspec, kspec, kspec, kspec,
                  pl.BlockSpec((1, nblk, SUBLANE, LANE), lambda b, p, i, sk: (b, 0, 0, 0))],
        out_specs=qspec,
        scratch_shapes=[pltpu.VMEM((2, blk, LANE), F32)] * 3,
    )
    return pl.pallas_call(
        _fox_prompt_body,
        out_shape=jax.ShapeDtypeStruct((nb, nt, D_FQK), BF16),
        grid_spec=grid_spec,
        compiler_params=_cparams(("parallel", "parallel", "parallel")),
        name="fox_prompt",
    )(skip, fq, qa, fkb, ka, fvb, base)


def _log_gamma(h):
    return math.log(1.0 - 2.0 ** (-5.0 - h))


def _group_norm_gate(o, w, rg):
    mu = jnp.mean(o, axis=-1, keepdims=True)
    var = jnp.mean(jnp.square(o - mu), axis=-1, keepdims=True)
    on = (o - mu) * lax.rsqrt(var + EPS)
    rg = rg.astype(F32)
    return (rg * jax.nn.sigmoid(rg)) * (on * w)


def _ret_prompt_body(q_ref, k_ref, v_ref, g_ref, w_ref, o_ref, st_ref, s_scr):
    t = pl.program_id(1)

    @pl.when(t == 0)
    def _():
        s_scr[...] = jnp.zeros_like(s_scr)

    L = RET_CHUNK
    ri = lax.broadcasted_iota(jnp.int32, (L, L), 0).astype(F32)
    ci = lax.broadcasted_iota(jnp.int32, (L, L), 1).astype(F32)
    diff = ri - ci
    rv = lax.broadcasted_iota(jnp.int32, (L, DV_R), 0).astype(F32)
    for c in range(q_ref.shape[1] // L):
        rows = slice(c * L, (c + 1) * L)
        for h in range(H_R):
            lg = _log_gamma(h)
            kc = slice(h * DK_R, (h + 1) * DK_R)
            vc = slice(h * DV_R, (h + 1) * DV_R)
            q = q_ref[0, rows, kc]
            k = k_ref[0, rows, kc]
            v = v_ref[0, rows, vc]
            decay = jnp.where(diff >= 0, jnp.exp(lg * jnp.maximum(diff, 0.0)), 0.0)
            scores = _dot_nt(q, k) * decay
            s0 = s_scr[h]
            o = _dot(scores.astype(BF16), v) + _dot(q, s0.astype(BF16)) * jnp.exp(lg * (rv + 1.0))
            kd = (k.astype(F32) * jnp.exp(lg * (L - 1.0 - ri))).astype(BF16)
            s_scr[h] = s0 * math.exp(lg * L) + _dot_tn(kd, v)
            o_ref[0, rows, vc] = _group_norm_gate(o, w_ref[:, vc], g_ref[0, rows, vc]).astype(o_ref.dtype)

    @pl.when(t == pl.num_programs(1) - 1)
    def _():
        st_ref[0] = s_scr[...]


def _ret_prompt(rq, rk, rv, rg, norm_w):
    nb, nt, _ = rq.shape
    tb = min(RET_BLOCK, nt)
    spec = lambda w: pl.BlockSpec((1, tb, w), lambda b, t: (b, t, 0))
    return pl.pallas_call(
        _ret_prompt_body,
        out_shape=[jax.ShapeDtypeStruct((nb, nt, D_RV), BF16),
                   jax.ShapeDtypeStruct((nb, H_R, DK_R, DV_R), F32)],
        grid=(nb, nt // tb),
        in_specs=[spec(D_RQK), spec(D_RQK), spec(D_RV), spec(D_RV), _const_spec((1, D_RV))],
        out_specs=[spec(D_RV), pl.BlockSpec((1, H_R, DK_R, DV_R), lambda b, t: (b, 0, 0, 0))],
        scratch_shapes=[pltpu.VMEM((H_R, DK_R, DV_R), F32)],
        compiler_params=_cparams(("parallel", "arbitrary")),
        name="ret_prompt",
    )(rq, rk, rv, rg, norm_w)


def _ret_decode_body(q_ref, k_ref, v_ref, g_ref, w_ref, s_ref, o_ref, so_ref):
    rr = lax.broadcasted_iota(jnp.int32, (DK_R, DK_R), 0)
    cc = lax.broadcasted_iota(jnp.int32, (DK_R, DK_R), 1)
    eye = rr == cc
    for n in range(q_ref.shape[0]):
        for h in range(H_R):
            gamma = math.exp(_log_gamma(h))
            kc = slice(h * DK_R, (h + 1) * DK_R)
            vc = slice(h * DV_R, (h + 1) * DV_R)
            q = q_ref[n, :, kc]
            k = k_ref[n, :, kc]
            v = v_ref[n, :, vc]
            s0 = s_ref[n, h]
            qk = jnp.sum(q.astype(F32) * k.astype(F32), axis=-1, keepdims=True)
            q_s = _dot(jnp.broadcast_to(q, (2 * SUBLANE, DK_R)), s0.astype(BF16))[:1]
            o = qk.astype(BF16).astype(F32) * v.astype(F32) + q_s * gamma
            k_diag = jnp.where(eye, jnp.broadcast_to(k.astype(F32), (DK_R, DK_R)), 0.0).astype(BF16)
            v_rows = jnp.broadcast_to(v.astype(F32), (DK_R, DV_R)).astype(BF16)
            so_ref[n, h] = s0 * gamma + _dot(k_diag, v_rows)
            o_ref[n, :, vc] = _group_norm_gate(o, w_ref[:, vc], g_ref[n, :, vc]).astype(o_ref.dtype)


def _ret_decode(rq, rk, rv, rg, norm_w, state):
    ns = rq.shape[0]
    sb = min(DEC_SEQS, ns)
    spec = lambda w: pl.BlockSpec((sb, 1, w), lambda i: (i, 0, 0))
    st_spec = pl.BlockSpec((sb, H_R, DK_R, DV_R), lambda i: (i, 0, 0, 0))
    return pl.pallas_call(
        _ret_decode_body,
        out_shape=[jax.ShapeDtypeStruct((ns, 1, D_RV), BF16), jax.ShapeDtypeStruct(state.shape, F32)],
        grid=(ns // sb,),
        in_specs=[spec(D_RQK), spec(D_RQK), spec(D_RV), spec(D_RV), _const_spec((1, D_RV)), st_spec],
        out_specs=[spec(D_RV), st_spec],
        compiler_params=_cparams(("parallel",)),
        name="ret_decode",
    )(rq, rk, rv, rg, norm_w, state)


def _page_totals_body(lf_ref, ones_ref, o_ref):
    ones = ones_ref[...]
    for c in range(o_ref.shape[0]):
        hi, mid, lo = _split3(lf_ref[c * LANE:(c + 1) * LANE, :])
        o_ref[c:c + 1, :] = (_dot_nt(ones, hi) + _dot_nt(ones, mid) + _dot_nt(ones, lo))[:1]


def _page_totals(clf_t):
    n_phys = clf_t.shape[0]
    rows = clf_t.reshape(n_phys * H_F, PAGE_SIZE)
    blk = SUBLANE * LANE
    n_rows = -(-rows.shape[0] // blk) * blk
    if n_rows != rows.shape[0]:
        rows = jnp.pad(rows, ((0, n_rows - rows.shape[0]), (0, 0)))
    step = max(d for d in range(blk, 8 * blk + 1, blk) if n_rows % d == 0)
    tot = pl.pallas_call(
        _page_totals_body,
        out_shape=jax.ShapeDtypeStruct((n_rows // LANE, LANE), F32),
        grid=(n_rows // step,),
        in_specs=[pl.BlockSpec((step, PAGE_SIZE), lambda i: (i, 0)), _const_spec((2 * SUBLANE, LANE))],
        out_specs=pl.BlockSpec((step // LANE, LANE), lambda i: (i, 0)),
        compiler_params=_cparams(("parallel",)),
        name="page_totals",
    )(rows, jnp.ones((2 * SUBLANE, LANE), BF16))
    return tot.reshape(-1)[:n_phys * H_F].reshape(n_phys, H_F)


def _fox_decode_body(pt_ref, need_ref, q_ref, kn_ref, vn_ref, lfn_ref, msuf_ref, ones_ref, pex_ref,
                     *rest, n_pages):
    k_refs = rest[:n_pages]
    v_refs = rest[n_pages:2 * n_pages]
    lf_refs = rest[2 * n_pages:3 * n_pages]
    o_ref = rest[3 * n_pages]
    m_run, l_run, tail, s_self, acc_run, qrep = rest[3 * n_pages + 1:]
    g = pl.program_id(1)
    need = need_ref[pl.program_id(0)]
    stat = (SUBLANE, LANE)
    grp = DH_F // SUBLANE

    def head_sums(x):
        return x.reshape(H_F, grp, SUBLANE, LANE).sum(axis=1).sum(axis=1)

    @pl.when(g == 0)
    def _():
        q = jnp.broadcast_to(q_ref[0], (D_FQK, LANE))
        qrep[...] = q
        ss = head_sums(q * kn_ref[0])
        s_self[...] = ss
        m_run[...] = ss
        l_run[...] = jnp.zeros(stat, F32)
        acc_run[...] = jnp.zeros_like(acc_run)
        tail[...] = jnp.broadcast_to(lfn_ref[0], stat)

    @pl.when(g * n_pages < need)
    def _():
        q = qrep[...]
        rows = [lf_refs[r][0] for r in range(n_pages)]
        if n_pages * SUBLANE < LANE:
            rows.append(jnp.zeros((LANE - n_pages * SUBLANE, LANE), F32))
        lf = jnp.concatenate(rows, axis=0)
        suf = _dot3_lhs(lf, msuf_ref[...])
        tot = _dot3_lhs(lf, ones_ref[...])
        pre = _dot3_rhs(pex_ref[...], tot)
        t_in = tail[...]
        last = slice((n_pages - 1) * SUBLANE, n_pages * SUBLANE)
        tail[...] = t_in + pre[last] + tot[last]
        s_pages = []
        m_step = jnp.full((SUBLANE, 1), NEG_INF, F32)
        for r in range(n_pages):
            pr = slice(r * SUBLANE, (r + 1) * SUBLANE)
            sr = head_sums(k_refs[r][0] * q) + (suf[pr] + pre[pr] + t_in)
            sr = jnp.where(g * n_pages + r < need, sr, NEG_INF)
            s_pages.append(sr)
            m_step = jnp.maximum(m_step, jnp.max(sr, axis=1, keepdims=True))
        m_prev = m_run[...]
        m_new = jnp.maximum(m_prev, m_step)
        alpha = jnp.exp(m_prev - m_new)
        p_pages = [jnp.exp(sr - m_new) for sr in s_pages]
        l_step = jnp.zeros((SUBLANE, 1), F32)
        for p in p_pages:
            l_step = l_step + jnp.sum(p, axis=1, keepdims=True)
        for h in range(H_F):
            hr = slice(h * DH_F, (h + 1) * DH_F)
            acc = acc_run[hr, :] * alpha[h:h + 1, :]
            for r in range(n_pages):
                acc = acc + v_refs[r][0, hr, :] * p_pages[r][h:h + 1, :]
            acc_run[hr, :] = acc
        l_run[...] = alpha * l_run[...] + l_step
        m_run[...] = m_new

    @pl.when(g == pl.num_programs(1) - 1)
    def _():
        w_self = jnp.exp(s_self[...] - m_run[...])
        l_tot = l_run[...] + w_self
        for h in range(H_F):
            hr = slice(h * DH_F, (h + 1) * DH_F)
            acc_col = jnp.sum(acc_run[hr, :], axis=1, keepdims=True)
            o_ref[0, hr, :] = (acc_col + w_self[h:h + 1, :1] * vn_ref[0, hr, :]) / l_tot[h:h + 1, :1]


def _fox_decode(page_table, fq_col, fk_col, fv_col, lfn_col, ck_t, cv_t, clf_t, thr):
    ns, n_tab = page_table.shape
    pg = min(DEC_PAGES, n_tab)
    assert pg * SUBLANE <= LANE and n_tab % pg == 0

    tot = _page_totals(clf_t)[page_table]
    after = lfn_col.reshape(ns, 1, H_F) + jnp.cumsum(tot[:, ::-1], axis=1)[:, ::-1] - tot
    need = jnp.maximum(jnp.sum(jnp.any(after >= -thr, axis=-1), axis=1), 1).astype(jnp.int32)

    r = np.arange(LANE)
    msuf = jnp.asarray((r[:, None] > r[None, :]).astype(np.float32), dtype=BF16)
    same_head = (r[:, None] % SUBLANE) == (r[None, :] % SUBLANE)
    in_step = (r[:, None] < pg * SUBLANE) & (r[None, :] < pg * SUBLANE)
    pex = jnp.asarray((same_head & in_step & (r[None, :] // SUBLANE < r[:, None] // SUBLANE)).astype(np.float32),
                      dtype=BF16)
    col = lambda w: pl.BlockSpec((1, w, 1), lambda s, g, pt, nd: (s, 0, 0))

    def page_spec(rows, idx):
        def index_map(s, g, pt, nd):
            pos = g * pg + idx
            n = nd[s]
            last = idx + pg * ((n - 1 - idx) // pg)
            eff = jnp.where(pos < n, pos, jnp.where(idx < n, last, idx))
            return (pt[s, n_tab - 1 - eff], 0, 0)
        return pl.BlockSpec((1, rows, PAGE_SIZE), index_map)

    stat = pltpu.VMEM((SUBLANE, LANE), F32)
    grid_spec = pltpu.PrefetchScalarGridSpec(
        num_scalar_prefetch=2,
        grid=(ns, n_tab // pg),
        in_specs=[col(D_FQK), col(D_FQK), col(D_FQK), col(H_F), _const_spec((LANE, LANE)),
                  _const_spec((LANE, LANE)), _const_spec((LANE, LANE)),
                  *[page_spec(D_FQK, i) for i in range(pg)], *[page_spec(D_FQK, i) for i in range(pg)],
                  *[page_spec(H_F, i) for i in range(pg)]],
        out_specs=col(D_FQK),
        scratch_shapes=[stat, stat, stat, stat, pltpu.VMEM((D_FQK, LANE), F32), pltpu.VMEM((D_FQK, LANE), F32)],
    )
    return pl.pallas_call(
        functools.partial(_fox_decode_body, n_pages=pg),
        out_shape=jax.ShapeDtypeStruct((ns, D_FQK, 1), F32),
        grid_spec=grid_spec,
        compiler_params=_cparams(("parallel", "arbitrary")),
        name="fox_decode",
    )(page_table, need, fq_col, fk_col, fv_col, lfn_col, msuf, jnp.ones((LANE, LANE), BF16), pex,
      *([ck_t] * pg), *([cv_t] * pg), *([clf_t] * pg))


def _outproj_body(x_ref, gt_ref, or_ref, of_ref, gr_ref, gf_ref, wb_ref, wo_ref, o_ref, *, per_row):
    gt = _mod_rows(gt_ref, per_row)
    merged = (gr_ref[0].astype(F32) * _dot(or_ref[0], wb_ref[:D_RV, :])
              + gf_ref[0].astype(F32) * _dot(of_ref[0], wb_ref[D_RV:, :]))
    y = _dot(merged.astype(BF16), wo_ref[...])
    o_ref[0] = x_ref[0] + gt * y


def _outproj(x, mod, per_row, row_off, o_r, o_f, gr, gf, w_branch, w_out):
    nb, nt, _ = x.shape
    tm = min(ROW_TILE, nt)
    row = lambda w: pl.BlockSpec((1, tm, w), lambda b, t: (b, t, 0))
    return pl.pallas_call(
        functools.partial(_outproj_body, per_row=per_row),
        out_shape=jax.ShapeDtypeStruct(x.shape, F32),
        grid=(nb, nt // tm),
        in_specs=[row(D_MODEL), _mod_specs(1, per_row, nt, row_off)[2], row(D_RV), row(D_FQK),
                  row(D_MODEL), row(D_MODEL), _const_spec((D_RV + D_FQK, D_MODEL)),
                  _const_spec((D_MODEL, D_MODEL))],
        out_specs=row(D_MODEL),
        compiler_params=_cparams(("parallel", "parallel")),
        name="outproj",
    )(x, mod, o_r, o_f, gr, gf, w_branch, w_out)


def _rope_tables(pos):
    half = DK_R // 2
    inv = ROPE_BASE ** (-jnp.arange(half, dtype=F32) / half)
    ang = pos.astype(F32)[:, None] * inv[None, :]
    cos, sin = jnp.cos(ang), jnp.sin(ang)
    return jnp.concatenate([cos, cos], axis=1), jnp.concatenate([-sin, sin], axis=1)


def _permute_w_in(w_in):
    sizes = (D_RQK, D_RQK, D_RV, D_RV, D_FQK, D_FQK, D_FQK, H_F, D_MODEL, D_MODEL)
    offs = [0]
    for s in sizes:
        offs.append(offs[-1] + s)
    part = lambda i: w_in[:, offs[i]:offs[i + 1]]
    fl = jnp.pad(part(7), ((0, 0), (0, LANE - H_F)))
    return jnp.concatenate([part(i) for i in (0, 1, 2, 3, 4, 5, 6, 8, 9)] + [fl], axis=1).astype(BF16)


def kernel(x_prompt, x_sample, cache_k, cache_v, cache_logf, state_ret, page_table, c_prompt, c_sample,
           w_ada, b_ada, w_ffa_up, w_ffa_down, w_in, b_forget, q_norm_w, k_norm_w, ret_norm_w,
           w_branch, w_out, w_ffb_up, w_ffb_down):
    nb, nt, _ = x_prompt.shape
    ns = x_sample.shape[0]
    depth = w_ada.shape[0]
    past_len = page_table.shape[1] * PAGE_SIZE
    assert x_sample.shape[1] == 1 and nb <= SUBLANE and ns % SUBLANE == 0 and nt % ATT_BLK == 0

    pad = (-(ns + SUBLANE)) % (2 * SUBLANE)
    c_all = jnp.concatenate([c_sample, c_prompt, jnp.zeros((SUBLANE - nb + pad, D_MODEL), F32)], axis=0)
    cos_p, sin_p = _rope_tables(jnp.arange(nt, dtype=jnp.int32))
    cos_s, sin_s = _rope_tables(jnp.full((ns,), past_len, jnp.int32))
    hd = np.arange(D_FQK) // DH_F
    bd = jnp.asarray((hd[:, None] == hd[None, :]).astype(np.float32) / DH_F, dtype=BF16)

    y_p = x_prompt
    y_s = x_sample.reshape(1, ns, D_MODEL)
    outs = [[] for _ in range(8)]
    for l in range(depth):
        bf = lambda w: w[l].astype(BF16)
        w_up_a, w_dn_a, w_up_b, w_dn_b = bf(w_ffa_up), bf(w_ffa_down), bf(w_ffb_up), bf(w_ffb_down)
        w_br, w_o = bf(w_branch), bf(w_out)
        w_inp = _permute_w_in(w_in[l])
        bf_row = jnp.pad(b_forget[l], (0, LANE - H_F)).reshape(1, LANE)
        qn = jnp.tile(q_norm_w[l], H_F).reshape(1, D_FQK)
        kn = jnp.tile(k_norm_w[l], H_F).reshape(1, D_FQK)
        rn = ret_norm_w[l].reshape(1, D_RV)
        mod = _adaln(c_all, bf(w_ada), b_ada[l])
        qk_bound = 1.02 * DH_F ** 0.5 * jnp.max(jnp.abs(q_norm_w[l])) * jnp.max(jnp.abs(k_norm_w[l]))
        thr = EXP_UNDERFLOW + 2.0 * qk_bound
        page_view = lambda c, w: jnp.transpose(c, (0, 2, 3, 1)).reshape(c.shape[0], w, PAGE_SIZE)

        y_p = _ffn(y_p, mod, 0, False, ns, w_up_a, w_dn_a)
        pj = _inproj(y_p, mod, False, ns, cos_p, sin_p, w_inp, bf_row, qn, kn, bd)
        qa, ka, base, skip = _forget_terms(pj["lfp"], ATT_BLK, thr)
        skip = skip[:, :, 0, :H_F].astype(jnp.int32).reshape(-1)
        o_f = _fox_prompt(skip, pj["fq"], qa, pj["fkb"], ka, pj["fvb"], base, ATT_BLK)
        o_r, st_p = _ret_prompt(pj["rq"], pj["rk"], pj["rv"], pj["rg"], rn)
        y_p = _outproj(y_p, mod, False, ns, o_r, o_f, pj["gr"], pj["gf"], w_br, w_o)
        y_p = _ffn(y_p, mod, 2, False, ns, w_up_b, w_dn_b)

        y_s = _ffn(y_s, mod, 0, True, 0, w_up_a, w_dn_a)
        sj = _inproj(y_s, mod, True, 0, cos_s, sin_s, w_inp, bf_row, qn, kn, bd)
        seq = lambda a: a.reshape(ns, 1, a.shape[-1])
        col = lambda a: jnp.transpose(a, (2, 1, 0))
        as_col = lambda a: a.astype(F32).reshape(ns, a.shape[-1], 1)
        o_f_s = _fox_decode(page_table, as_col(sj["fq"]), as_col(sj["fkb"]), col(sj["fvt"]), col(sj["lft"]),
                            page_view(cache_k[l], D_FQK), page_view(cache_v[l], D_FQK),
                            jnp.transpose(cache_logf[l], (0, 2, 1)), thr)
        o_r_s, st_s = _ret_decode(seq(sj["rq"]), seq(sj["rk"]), seq(sj["rv"]), seq(sj["rg"]), rn, state_ret[l])
        y_s = _outproj(y_s, mod, True, 0, o_r_s.reshape(1, ns, D_RV), o_f_s.reshape(1, ns, D_FQK).astype(BF16),
                       sj["gr"], sj["gf"], w_br, w_o)
        y_s = _ffn(y_s, mod, 2, True, 0, w_up_b, w_dn_b)

        rows = lambda a: jnp.transpose(a.reshape(a.shape[0], H_F, DH_F, a.shape[2]), (0, 3, 1, 2))
        for dst, val in zip(outs, (rows(pj["fkt"]), rows(pj["fvt"]), jnp.transpose(pj["lft"], (0, 2, 1)), st_p,
                                   rows(sj["fkt"]).reshape(ns, 1, H_F, DH_F),
                                   rows(sj["fvt"]).reshape(ns, 1, H_F, DH_F),
                                   jnp.transpose(sj["lft"], (0, 2, 1)).reshape(ns, 1, H_F), st_s)):
            dst.append(val)

    return (y_p, y_s.reshape(ns, 1, D_MODEL), *[jnp.stack(o) for o in outs])
```

```python
import functools
import math

import jax
import jax.numpy as jnp
import numpy as np
from jax import lax
from jax.experimental import pallas as pl
from jax.experimental.pallas import tpu as pltpu

D_MODEL = 1024
PAGE_SIZE = 128
H_R, DK_R, DV_R = 4, 128, 256
RET_CHUNK = 128
ROPE_BASE = 10000.0
H_F, DH_F = 8, 64
D_FF = 2816
EPS = 1e-6
N_SUB = 3
NEG_INF = -1e30
EXP_UNDERFLOW = 110.0

D_RQK = H_R * DK_R
D_RV = H_R * DV_R
D_FQK = H_F * DH_F
N_PAIR = H_F // 2
LANE = 128
SUBLANE = 8

F32 = jnp.float32
BF16 = jnp.bfloat16

_SEG = {}
_off = 0
for _name, _w in (("rq", D_RQK), ("rk", D_RQK), ("rv", D_RV), ("rg", D_RV), ("fq", D_FQK), ("fk", D_FQK),
                  ("fv", D_FQK), ("gr", D_MODEL), ("gf", D_MODEL), ("fl", LANE)):
    _SEG[_name] = (_off, _off + _w)
    _off += _w
D_IN_PAD = _off

ROW_TILE = 512
ATT_BLK = 512
FFN_TILE = 1024
FF_CHUNK = 256
RET_BLOCK = 512
DEC_PAGES = 4
DEC_SEQS = 4
VMEM_LIMIT = 56 * 2 ** 20


def _cparams(sem):
    return pltpu.CompilerParams(dimension_semantics=sem, vmem_limit_bytes=VMEM_LIMIT)


def _const_spec(shape):
    nd = len(shape)
    return pl.BlockSpec(shape, lambda *_: (0,) * nd, pipeline_mode=pl.Buffered(1))


def _dot(a, b):
    return jnp.dot(a, b, preferred_element_type=F32)


def _dot_nt(a, b):
    return lax.dot_general(a, b, (((1,), (1,)), ((), ())), preferred_element_type=F32)


def _dot_tn(a, b):
    return lax.dot_general(a, b, (((0,), (0,)), ((), ())), preferred_element_type=F32)


def _split3(x):
    hi = x.astype(BF16)
    r1 = x - hi.astype(F32)
    mid = r1.astype(BF16)
    lo = (r1 - mid.astype(F32)).astype(BF16)
    return hi, mid, lo


def _dot3_rhs(a_bf16, x):
    hi, mid, lo = _split3(x)
    return _dot(a_bf16, hi) + _dot(a_bf16, mid) + _dot(a_bf16, lo)


def _dot3_lhs(x, b_bf16):
    hi, mid, lo = _split3(x)
    return _dot(hi, b_bf16) + _dot(mid, b_bf16) + _dot(lo, b_bf16)


def _log_sigmoid(x):
    return jnp.minimum(x, 0.0) - jnp.log1p(jnp.exp(-jnp.abs(x)))


def _mod_rows(ref, per_row):
    if per_row:
        return ref[0]
    return ref[0, pl.ds(pl.program_id(0), 1), :]


def _modulated_norm(x, shift, scale):
    ms = jnp.mean(x * x, axis=-1, keepdims=True)
    return (x * lax.rsqrt(ms + EPS)) * (1.0 + scale) + shift


def _mod_specs(sub, per_row, n_rows, row_off):
    specs = []
    for k in range(3):
        j = sub * 3 + k
        if per_row:
            specs.append(pl.BlockSpec((1, n_rows, D_MODEL), lambda b, t, j=j: (j, 0, 0)))
        else:
            specs.append(pl.BlockSpec((1, SUBLANE, D_MODEL), lambda b, t, j=j: (j, row_off // SUBLANE, 0)))
    return specs


def _adaln_body(c_ref, w_ref, b_ref, o_ref):
    c = c_ref[...]
    s = c * jax.nn.sigmoid(c)
    o_ref[0] = _dot(s.astype(BF16), w_ref[...]) + b_ref[...]


def _adaln(c_all, w_ada, b_ada):
    rows = c_all.shape[0]
    n = N_SUB * 3
    return pl.pallas_call(
        _adaln_body,
        out_shape=jax.ShapeDtypeStruct((n, rows, D_MODEL), F32),
        grid=(n,),
        in_specs=[pl.BlockSpec((rows, D_MODEL), lambda j: (0, 0)),
                  pl.BlockSpec((D_MODEL, D_MODEL), lambda j: (0, j)),
                  pl.BlockSpec((1, D_MODEL), lambda j: (0, j))],
        out_specs=pl.BlockSpec((1, rows, D_MODEL), lambda j: (j, 0, 0)),
        compiler_params=_cparams(("parallel",)),
        name="adaln",
    )(c_all, w_ada, b_ada.reshape(1, -1))


def _ffn_body(x_ref, sh_ref, sc_ref, gt_ref, wup_ref, wdn_ref, o_ref, *, per_row):
    x = x_ref[0]
    sh, sc, gt = (_mod_rows(r, per_row) for r in (sh_ref, sc_ref, gt_ref))
    h = _modulated_norm(x, sh, sc).astype(BF16)
    acc = None
    for c in range(D_FF // FF_CHUNK):
        lo = c * FF_CHUNK
        g = _dot(h, wup_ref[:, lo:lo + FF_CHUNK])
        u = _dot(h, wup_ref[:, D_FF + lo:D_FF + lo + FF_CHUNK])
        a = (g * jax.nn.sigmoid(g) * u).astype(BF16)
        part = _dot(a, wdn_ref[lo:lo + FF_CHUNK, :])
        acc = part if acc is None else acc + part
    o_ref[0] = x + (0.5 * gt) * acc


def _ffn(x, mod, sub, per_row, row_off, w_up, w_dn):
    nb, nt, _ = x.shape
    tm = min(FFN_TILE, nt)
    x_spec = pl.BlockSpec((1, tm, D_MODEL), lambda b, t: (b, t, 0))
    return pl.pallas_call(
        functools.partial(_ffn_body, per_row=per_row),
        out_shape=jax.ShapeDtypeStruct(x.shape, F32),
        grid=(nb, nt // tm),
        in_specs=[x_spec, *_mod_specs(sub, per_row, nt, row_off),
                  _const_spec((D_MODEL, 2 * D_FF)), _const_spec((D_FF, D_MODEL))],
        out_specs=x_spec,
        compiler_params=_cparams(("parallel", "parallel")),
        name="ffn",
    )(x, mod, mod, mod, w_up, w_dn)


def _inproj_body(x_ref, sh_ref, sc_ref, cos_ref, sin_ref, w_ref, bf_ref, qn_ref, kn_ref, bd_ref,
                 rq_o, rk_o, rv_o, rg_o, fq_o, fkb_o, fvb_o, fkt_o, fvt_o, gr_o, gf_o, lft_o, lfp_o, *, per_row):
    x = x_ref[0]
    sh, sc = (_mod_rows(r, per_row) for r in (sh_ref, sc_ref))
    h = _modulated_norm(x, sh, sc).astype(BF16)

    def seg(name):
        lo, hi = _SEG[name]
        return _dot(h, w_ref[:, lo:hi])

    cos, sin = cos_ref[...], sin_ref[...]

    def rope_store(z, out, scale):
        for hd in range(H_R):
            zh = z[:, hd * DK_R:(hd + 1) * DK_R]
            r = zh * cos + pltpu.roll(zh, DK_R // 2, 1) * sin
            if scale != 1.0:
                r = r * scale
            out[0, :, hd * DK_R:(hd + 1) * DK_R] = r.astype(out.dtype)

    rope_store(seg("rq"), rq_o, 1.0)
    rope_store(seg("rk"), rk_o, DK_R ** -0.5)
    rv_o[0] = seg("rv").astype(BF16)
    rg_o[0] = seg("rg").astype(BF16)

    bd = bd_ref[...]

    def head_rms(z, w):
        z2 = z * z
        hi = z2.astype(BF16)
        lo = (z2 - hi.astype(F32)).astype(BF16)
        ms = _dot(hi, bd) + _dot(lo, bd)
        return (z * lax.rsqrt(ms + EPS)) * w

    fq_o[0] = (head_rms(seg("fq"), qn_ref[...]) * (DH_F ** -0.5)).astype(BF16)
    fk = head_rms(seg("fk"), kn_ref[...])
    fkt_o[0] = fk.T
    fkb_o[0] = fk.astype(BF16)
    fv = seg("fv")
    fvt_o[0] = fv.T
    fvb_o[0] = fv.astype(BF16)
    gr_o[0] = jax.nn.sigmoid(seg("gr")).astype(BF16)
    gf_o[0] = jax.nn.sigmoid(seg("gf")).astype(BF16)

    logf = _log_sigmoid(seg("fl") + bf_ref[...])
    lane = lax.broadcasted_iota(jnp.int32, logf.shape, 1)
    logf = jnp.where(lane < H_F, logf, 0.0)
    lfp_o[0] = logf
    lft_o[0] = logf.T[:H_F]


def _inproj(x, mod, per_row, row_off, cos_t, sin_t, w_in, bf_row, qn, kn, bd):
    nb, nt, _ = x.shape
    tm = min(ROW_TILE, nt)
    row = lambda w: pl.BlockSpec((1, tm, w), lambda b, t: (b, t, 0))
    col = lambda w: pl.BlockSpec((1, w, tm), lambda b, t: (b, 0, t))
    outs = [("rq", D_RQK, BF16), ("rk", D_RQK, BF16), ("rv", D_RV, BF16), ("rg", D_RV, BF16),
            ("fq", D_FQK, BF16), ("fkb", D_FQK, BF16), ("fvb", D_FQK, BF16), ("fkt", D_FQK, None),
            ("fvt", D_FQK, None), ("gr", D_MODEL, BF16), ("gf", D_MODEL, BF16), ("lft", H_F, None),
            ("lfp", LANE, F32)]
    shp = lambda w, dt: jax.ShapeDtypeStruct((nb, w, nt) if dt is None else (nb, nt, w), dt or F32)
    res = pl.pallas_call(
        functools.partial(_inproj_body, per_row=per_row),
        out_shape=[shp(w, dt) for _, w, dt in outs],
        grid=(nb, nt // tm),
        in_specs=[row(D_MODEL), *_mod_specs(1, per_row, nt, row_off)[:2],
                  pl.BlockSpec((tm, DK_R), lambda b, t: (t, 0)), pl.BlockSpec((tm, DK_R), lambda b, t: (t, 0)),
                  _const_spec((D_MODEL, D_IN_PAD)), _const_spec((1, LANE)), _const_spec((1, D_FQK)),
                  _const_spec((1, D_FQK)), _const_spec((D_FQK, D_FQK))],
        out_specs=[col(w) if dt is None else row(w) for _, w, dt in outs],
        compiler_params=_cparams(("parallel", "parallel")),
        name="inproj",
    )(x, mod, mod, cos_t, sin_t, w_in, bf_row, qn, kn, bd)
    return {name: r for (name, _, _), r in zip(outs, res)}


def _forget_terms_body(lf_ref, ltri_ref, ones_ref, pq_ref, pk_ref, qrow_ref, krow_ref, thr_ref,
                       qa_o, ka_o, base_o, skip_o, carry, bases):
    i = pl.program_id(1)

    @pl.when(i == 0)
    def _():
        carry[...] = jnp.zeros_like(carry)
        bases[...] = jnp.zeros_like(bases)

    ltri = ltri_ref[...]
    run = jnp.zeros((1, LANE), F32)
    for r in range(lf_ref.shape[1] // LANE):
        rows = slice(r * LANE, (r + 1) * LANE)
        c = _dot3_rhs(ltri, lf_ref[0, rows, :]) + run
        run = c[LANE - 1:LANE, :]
        hi, mid, lo = _split3(c)
        qa = _dot(hi, pq_ref[0]) + _dot(mid, pq_ref[1]) + _dot(lo, pq_ref[2]) + qrow_ref[...]
        ka = _dot(hi, pk_ref[0]) + _dot(mid, pk_ref[1]) + _dot(lo, pk_ref[2]) + krow_ref[...]
        qa_o[0, rows, :] = qa.astype(BF16)
        ka_o[0, rows, :] = ka.astype(BF16)

    base = carry[...]
    rr = lax.broadcasted_iota(jnp.int32, (2 * SUBLANE, LANE), 0)
    cc = lax.broadcasted_iota(jnp.int32, (2 * SUBLANE, LANE), 1)
    diag = jnp.where(rr == cc, jnp.broadcast_to(base, (2 * SUBLANE, LANE)), 0.0)
    base_o[0, 0] = _dot3_lhs(diag, ones_ref[...])[:SUBLANE]
    carry[...] = base + run

    bases[pl.ds(i, 1), :] = base
    jj = lax.broadcasted_iota(jnp.int32, bases.shape, 0)
    far = (jj >= 1) & (jj <= i) & (bases[...] - base > thr_ref[...])
    skip_o[0, 0] = jnp.broadcast_to(jnp.sum(jnp.where(far, 1.0, 0.0), axis=0, keepdims=True), (SUBLANE, LANE))


def _forget_tables(blk):
    w = N_PAIR * LANE
    pq, pk = np.zeros((3, LANE, w), np.float32), np.zeros((3, LANE, w), np.float32)
    qrow, krow = np.zeros((1, w), np.float32), np.zeros((1, w), np.float32)
    for h in range(H_F):
        l0 = (h // 2) * LANE + (h % 2) * DH_F
        for s in range(3):
            pq[s, h, l0 + s] = 1.0
            pk[s, h, l0 + 3 + s] = -1.0
            qrow[0, l0 + 3 + s] = 1.0
            krow[0, l0 + s] = 1.0
    r = np.arange(blk)
    ltri = (r[None, :] <= r[:, None]).astype(np.float32)
    as_bf16 = lambda a: jnp.asarray(a, dtype=BF16)
    return (as_bf16(ltri), jnp.ones((LANE, LANE), BF16), as_bf16(pq), as_bf16(pk), jnp.asarray(qrow),
            jnp.asarray(krow))


def _forget_terms(lfp, blk, thr):
    nb, nt, _ = lfp.shape
    nblk = nt // blk
    w = N_PAIR * LANE
    ltri, ones, pq, pk, qrow, krow = _forget_tables(LANE)
    stat = jax.ShapeDtypeStruct((nb, nblk, SUBLANE, LANE), F32)
    stat_spec = pl.BlockSpec((1, 1, SUBLANE, LANE), lambda b, i: (b, i, 0, 0))
    return pl.pallas_call(
        _forget_terms_body,
        out_shape=[jax.ShapeDtypeStruct((nb, nt, w), BF16), jax.ShapeDtypeStruct((nb, nt, w), BF16), stat, stat],
        grid=(nb, nblk),
        in_specs=[pl.BlockSpec((1, blk, LANE), lambda b, i: (b, i, 0)),
                  _const_spec((LANE, LANE)), _const_spec((LANE, LANE)), _const_spec((3, LANE, w)),
                  _const_spec((3, LANE, w)), _const_spec((1, w)), _const_spec((1, w)), _const_spec((1, LANE))],
        out_specs=[pl.BlockSpec((1, blk, w), lambda b, i: (b, i, 0)),
                   pl.BlockSpec((1, blk, w), lambda b, i: (b, i, 0)), stat_spec, stat_spec],
        scratch_shapes=[pltpu.VMEM((1, LANE), F32), pltpu.VMEM((-(-nblk // SUBLANE) * SUBLANE, LANE), F32)],
        compiler_params=_cparams(("parallel", "arbitrary")),
        name="forget_terms",
    )(lfp, ltri, ones, pq, pk, qrow, krow, jnp.broadcast_to(thr.astype(F32), (1, LANE)))


def _fox_prompt_body(skip_ref, q_ref, qa_ref, k_ref, ka_ref, v_ref, base_ref, o_ref, m_scr, l_scr, acc_scr):
    b = pl.program_id(0)
    hp = pl.program_id(1)
    i = pl.program_id(2)
    tq = q_ref.shape[1]
    tk = tq
    lane2 = lax.broadcasted_iota(jnp.int32, (1, 2 * LANE), 1)
    first = (lane2 % LANE) < DH_F
    qf = jnp.concatenate([q_ref[0], qa_ref[0]], axis=1)
    zero = jnp.zeros_like(qf)
    q_heads = (jnp.where(first, qf, zero), jnp.where(first, zero, qf))

    m_scr[...] = jnp.full(m_scr.shape, NEG_INF, F32)
    l_scr[...] = jnp.zeros_like(l_scr)
    acc_scr[...] = jnp.zeros_like(acc_scr)

    def step(hh, j, masked):
        off = pl.multiple_of(j * tk, tk)
        kf = jnp.concatenate([k_ref[0, pl.ds(off, tk), :], ka_ref[0, pl.ds(off, tk), :]], axis=1)
        s = _dot_nt(q_heads[hh], kf)
        if masked:
            row = lax.broadcasted_iota(jnp.int32, (tq, tk), 0)
            col = lax.broadcasted_iota(jnp.int32, (tq, tk), 1)
            s = jnp.where(col <= row, s, NEG_INF)
        h = 2 * hp + hh
        d = base_ref[0, i, pl.ds(h, 1), :] - base_ref[0, j, pl.ds(h, 1), :]
        m_prev = m_scr[hh]
        m_next = jnp.maximum(m_prev, jnp.max(s, axis=1, keepdims=True) + d)
        alpha = jnp.exp(m_prev - m_next)
        sub = m_next - d
        p = jnp.exp(s - jnp.concatenate([sub] * (tk // LANE), axis=1))
        l_scr[hh] = alpha * l_scr[hh] + jnp.sum(p, axis=1, keepdims=True)
        acc_scr[hh] = alpha * acc_scr[hh] + _dot(p.astype(BF16), v_ref[0, pl.ds(off, tk), :])
        m_scr[hh] = m_next

    for hh in range(2):
        j0 = skip_ref[(b * pl.num_programs(2) + i) * H_F + 2 * hp + hh]

        def full_step(j, carry, hh=hh):
            step(hh, j, False)
            return carry

        lax.fori_loop(j0, i, full_step, 0)
        step(hh, i, True)

    lane = lax.broadcasted_iota(jnp.int32, (1, LANE), 1)
    o = jnp.where(lane < DH_F, acc_scr[0] / l_scr[0], acc_scr[1] / l_scr[1])
    o_ref[0] = o.astype(o_ref.dtype)


def _fox_prompt(skip, fq, qa, fkb, ka, fvb, base, blk):
    nb, nt, _ = fq.shape
    nblk = nt // blk
    qspec = pl.BlockSpec((1, blk, LANE), lambda b, p, i, sk: (b, i, p))
    kspec = pl.BlockSpec((1, nt, LANE), lambda b, p, i, sk: (b, 0, p))
    grid_spec = pltpu.PrefetchScalarGridSpec(
        num_scalar_prefetch=1,
        grid=(nb, N_PAIR, nblk),
        in_specs=[qspec, qspec, kspec, kspec, kspec,
                  pl.BlockSpec((1, nblk, SUBLANE, LANE), lambda b, p, i, sk: (b, 0, 0, 0))],
        out_specs=qspec,
        scratch_shapes=[pltpu.VMEM((2, blk, LANE), F32)] * 3,
    )
    return pl.pallas_call(
        _fox_prompt_body,
        out_shape=jax.ShapeDtypeStruct((nb, nt, D_FQK), BF16),
        grid_spec=grid_spec,
        compiler_params=_cparams(("parallel", "parallel", "parallel")),
        name="fox_prompt",
    )(skip, fq, qa, fkb, ka, fvb, base)


def _log_gamma(h):
    return math.log(1.0 - 2.0 ** (-5.0 - h))


def _group_norm_gate(o, w, rg):
    mu = jnp.mean(o, axis=-1, keepdims=True)
    var = jnp.mean(jnp.square(o - mu), axis=-1, keepdims=True)
    on = (o - mu) * lax.rsqrt(var + EPS)
    rg = rg.astype(F32)
    return (rg * jax.nn.sigmoid(rg)) * (on * w)


def _ret_prompt_body(q_ref, k_ref, v_ref, g_ref, w_ref, o_ref, st_ref, s_scr):
    t = pl.program_id(1)

    @pl.when(t == 0)
    def _():
        s_scr[...] = jnp.zeros_like(s_scr)

    L = RET_CHUNK
    ri = lax.broadcasted_iota(jnp.int32, (L, L), 0).astype(F32)
    ci = lax.broadcasted_iota(jnp.int32, (L, L), 1).astype(F32)
    diff = ri - ci
    rv = lax.broadcasted_iota(jnp.int32, (L, DV_R), 0).astype(F32)
    for c in range(q_ref.shape[1] // L):
        rows = slice(c * L, (c + 1) * L)
        for h in range(H_R):
            lg = _log_gamma(h)
            kc = slice(h * DK_R, (h + 1) * DK_R)
            vc = slice(h * DV_R, (h + 1) * DV_R)
            q = q_ref[0, rows, kc]
            k = k_ref[0, rows, kc]
            v = v_ref[0, rows, vc]
            decay = jnp.where(diff >= 0, jnp.exp(lg * jnp.maximum(diff, 0.0)), 0.0)
            scores = _dot_nt(q, k) * decay
            s0 = s_scr[h]
            o = _dot(scores.astype(BF16), v) + _dot(q, s0.astype(BF16)) * jnp.exp(lg * (rv + 1.0))
            kd = (k.astype(F32) * jnp.exp(lg * (L - 1.0 - ri))).astype(BF16)
            s_scr[h] = s0 * math.exp(lg * L) + _dot_tn(kd, v)
            o_ref[0, rows, vc] = _group_norm_gate(o, w_ref[:, vc], g_ref[0, rows, vc]).astype(o_ref.dtype)

    @pl.when(t == pl.num_programs(1) - 1)
    def _():
        st_ref[0] = s_scr[...]


def _ret_prompt(rq, rk, rv, rg, norm_w):
    nb, nt, _ = rq.shape
    tb = min(RET_BLOCK, nt)
    spec = lambda w: pl.BlockSpec((1, tb, w), lambda b, t: (b, t, 0))
    return pl.pallas_call(
        _ret_prompt_body,
        out_shape=[jax.ShapeDtypeStruct((nb, nt, D_RV), BF16),
                   jax.ShapeDtypeStruct((nb, H_R, DK_R, DV_R), F32)],
        grid=(nb, nt // tb),
        in_specs=[spec(D_RQK), spec(D_RQK), spec(D_RV), spec(D_RV), _const_spec((1, D_RV))],
        out_specs=[spec(D_RV), pl.BlockSpec((1, H_R, DK_R, DV_R), lambda b, t: (b, 0, 0, 0))],
        scratch_shapes=[pltpu.VMEM((H_R, DK_R, DV_R), F32)],
        compiler_params=_cparams(("parallel", "arbitrary")),
        name="ret_prompt",
    )(rq, rk, rv, rg, norm_w)


def _ret_decode_body(q_ref, k_ref, v_ref, g_ref, w_ref, s_ref, o_ref, so_ref):
    rr = lax.broadcasted_iota(jnp.int32, (DK_R, DK_R), 0)
    cc = lax.broadcasted_iota(jnp.int32, (DK_R, DK_R), 1)
    eye = rr == cc
    for n in range(q_ref.shape[0]):
        for h in range(H_R):
            gamma = math.exp(_log_gamma(h))
            kc = slice(h * DK_R, (h + 1) * DK_R)
            vc = slice(h * DV_R, (h + 1) * DV_R)
            q = q_ref[n, :, kc]
            k = k_ref[n, :, kc]
            v = v_ref[n, :, vc]
            s0 = s_ref[n, h]
            qk = jnp.sum(q.astype(F32) * k.astype(F32), axis=-1, keepdims=True)
            q_s = _dot(jnp.broadcast_to(q, (2 * SUBLANE, DK_R)), s0.astype(BF16))[:1]
            o = qk.astype(BF16).astype(F32) * v.astype(F32) + q_s * gamma
            k_diag = jnp.where(eye, jnp.broadcast_to(k.astype(F32), (DK_R, DK_R)), 0.0).astype(BF16)
            v_rows = jnp.broadcast_to(v.astype(F32), (DK_R, DV_R)).astype(BF16)
            so_ref[n, h] = s0 * gamma + _dot(k_diag, v_rows)
            o_ref[n, :, vc] = _group_norm_gate(o, w_ref[:, vc], g_ref[n, :, vc]).astype(o_ref.dtype)


def _ret_decode(rq, rk, rv, rg, norm_w, state):
    ns = rq.shape[0]
    sb = min(DEC_SEQS, ns)
    spec = lambda w: pl.BlockSpec((sb, 1, w), lambda i: (i, 0, 0))
    st_spec = pl.BlockSpec((sb, H_R, DK_R, DV_R), lambda i: (i, 0, 0, 0))
    return pl.pallas_call(
        _ret_decode_body,
        out_shape=[jax.ShapeDtypeStruct((ns, 1, D_RV), BF16), jax.ShapeDtypeStruct(state.shape, F32)],
        grid=(ns // sb,),
        in_specs=[spec(D_RQK), spec(D_RQK), spec(D_RV), spec(D_RV), _const_spec((1, D_RV)), st_spec],
        out_specs=[spec(D_RV), st_spec],
        compiler_params=_cparams(("parallel",)),
        name="ret_decode",
    )(rq, rk, rv, rg, norm_w, state)


def _page_totals_body(lf_ref, ones_ref, o_ref):
    ones = ones_ref[...]
    for c in range(o_ref.shape[0]):
        hi, mid, lo = _split3(lf_ref[c * LANE:(c + 1) * LANE, :])
        o_ref[c:c + 1, :] = (_dot_nt(ones, hi) + _dot_nt(ones, mid) + _dot_nt(ones, lo))[:1]


def _page_totals(clf_t):
    n_phys = clf_t.shape[0]
    rows = clf_t.reshape(n_phys * H_F, PAGE_SIZE)
    blk = SUBLANE * LANE
    n_rows = -(-rows.shape[0] // blk) * blk
    if n_rows != rows.shape[0]:
        rows = jnp.pad(rows, ((0, n_rows - rows.shape[0]), (0, 0)))
    step = max(d for d in range(blk, 8 * blk + 1, blk) if n_rows % d == 0)
    tot = pl.pallas_call(
        _page_totals_body,
        out_shape=jax.ShapeDtypeStruct((n_rows // LANE, LANE), F32),
        grid=(n_rows // step,),
        in_specs=[pl.BlockSpec((step, PAGE_SIZE), lambda i: (i, 0)), _const_spec((2 * SUBLANE, LANE))],
        out_specs=pl.BlockSpec((step // LANE, LANE), lambda i: (i, 0)),
        compiler_params=_cparams(("parallel",)),
        name="page_totals",
    )(rows, jnp.ones((2 * SUBLANE, LANE), BF16))
    return tot.reshape(-1)[:n_phys * H_F].reshape(n_phys, H_F)


def _fox_decode_body(pt_ref, ng_ref, q_ref, kn_ref, vn_ref, lfn_ref, msuf_ref, ones_ref, pex_ref,
                     ck_hbm, cv_hbm, clf_hbm, o_ref,
                     kbuf, vbuf, lfbuf, sem, m_run, l_run, tail, acc_run, qrep, count, *, n_pages, n_tab):
    s = pl.program_id(0)
    ng = ng_ref[s]
    stat = (SUBLANE, LANE)

    def page_copies(seq, group, slot):
        copies = []
        for r in range(n_pages):
            page = pt_ref[seq, n_tab - 1 - (group * n_pages + r)]
            for kind, (hbm, buf) in enumerate(((ck_hbm, kbuf), (cv_hbm, vbuf), (clf_hbm, lfbuf))):
                copies.append(pltpu.make_async_copy(hbm.at[page], buf.at[slot, r], sem.at[slot, kind]))
        return copies

    def head_sums(x):
        return x.reshape(H_F, DH_F // SUBLANE, SUBLANE, LANE).sum(axis=1).sum(axis=1)

    @pl.when(s == 0)
    def _():
        count[0] = 0
        for c in page_copies(0, 0, 0):
            c.start()

    first = count[0]
    q = jnp.broadcast_to(q_ref[0], (D_FQK, LANE))
    qrep[...] = q
    s_self = head_sums(q * kn_ref[0])
    m_run[...] = s_self
    l_run[...] = jnp.zeros(stat, F32)
    acc_run[...] = jnp.zeros_like(acc_run)
    tail[...] = jnp.broadcast_to(lfn_ref[0], stat)

    def group_step(g, carry):
        slot = lax.rem(first + g, 2)

        @pl.when(g + 1 < ng)
        def _():
            for c in page_copies(s, g + 1, 1 - slot):
                c.start()

        @pl.when(jnp.logical_and(g + 1 == ng, s + 1 < pl.num_programs(0)))
        def _():
            for c in page_copies(s + 1, 0, 1 - slot):
                c.start()

        for c in page_copies(s, g, slot):
            c.wait()

        qv = qrep[...]
        rows = [lfbuf[slot, r] for r in range(n_pages)]
        rows.append(jnp.zeros((LANE - n_pages * SUBLANE, LANE), F32))
        lf = jnp.concatenate(rows, axis=0)
        suf = _dot3_lhs(lf, msuf_ref[...])
        tot = _dot3_lhs(lf, ones_ref[...])
        pre = _dot3_rhs(pex_ref[...], tot)
        t_in = tail[...]
        last = slice((n_pages - 1) * SUBLANE, n_pages * SUBLANE)
        tail[...] = t_in + pre[last] + tot[last]
        s_pages = []
        m_step = jnp.full((SUBLANE, 1), NEG_INF, F32)
        for r in range(n_pages):
            pr = slice(r * SUBLANE, (r + 1) * SUBLANE)
            sr = head_sums(kbuf[slot, r] * qv) + (suf[pr] + pre[pr] + t_in)
            s_pages.append(sr)
            m_step = jnp.maximum(m_step, jnp.max(sr, axis=1, keepdims=True))
        m_prev = m_run[...]
        m_new = jnp.maximum(m_prev, m_step)
        alpha = jnp.exp(m_prev - m_new)
        p_pages = [jnp.exp(sr - m_new) for sr in s_pages]
        l_step = jnp.zeros((SUBLANE, 1), F32)
        for p in p_pages:
            l_step = l_step + jnp.sum(p, axis=1, keepdims=True)
        for h in range(H_F):
            hr = slice(h * DH_F, (h + 1) * DH_F)
            acc = acc_run[hr, :] * alpha[h:h + 1, :]
            for r in range(n_pages):
                acc = acc + vbuf[slot, r, hr, :] * p_pages[r][h:h + 1, :]
            acc_run[hr, :] = acc
        l_run[...] = alpha * l_run[...] + l_step
        m_run[...] = m_new
        return carry

    lax.fori_loop(0, ng, group_step, 0)
    count[0] = first + ng

    w_self = jnp.exp(s_self - m_run[...])
    l_tot = l_run[...] + w_self
    for h in range(H_F):
        hr = slice(h * DH_F, (h + 1) * DH_F)
        acc_col = jnp.sum(acc_run[hr, :], axis=1, keepdims=True)
        o_ref[0, hr, :] = (acc_col + w_self[h:h + 1, :1] * vn_ref[0, hr, :]) / l_tot[h:h + 1, :1]


def _fox_decode(page_table, fq_col, fk_col, fv_col, lfn_col, ck_t, cv_t, clf_t, thr):
    ns, n_tab = page_table.shape
    pg = min(DEC_PAGES, n_tab)
    assert pg * SUBLANE <= LANE and n_tab % pg == 0

    tot = _page_totals(clf_t)[page_table]
    after = lfn_col.reshape(ns, 1, H_F) + jnp.cumsum(tot[:, ::-1], axis=1)[:, ::-1] - tot
    need = jnp.maximum(jnp.sum(jnp.any(after >= -thr, axis=-1), axis=1), 1)
    n_groups = ((need + pg - 1) // pg).astype(jnp.int32)

    r = np.arange(LANE)
    msuf = jnp.asarray((r[:, None] > r[None, :]).astype(np.float32), dtype=BF16)
    same_head = (r[:, None] % SUBLANE) == (r[None, :] % SUBLANE)
    in_step = (r[:, None] < pg * SUBLANE) & (r[None, :] < pg * SUBLANE)
    pex = jnp.asarray((same_head & in_step & (r[None, :] // SUBLANE < r[:, None] // SUBLANE)).astype(np.float32),
                      dtype=BF16)
    col = lambda w: pl.BlockSpec((1, w, 1), lambda s, pt, ng: (s, 0, 0))
    hbm = pl.BlockSpec(memory_space=pl.ANY)
    stat = pltpu.VMEM((SUBLANE, LANE), F32)
    grid_spec = pltpu.PrefetchScalarGridSpec(
        num_scalar_prefetch=2,
        grid=(ns,),
        in_specs=[col(D_FQK), col(D_FQK), col(D_FQK), col(H_F), _const_spec((LANE, LANE)),
                  _const_spec((LANE, LANE)), _const_spec((LANE, LANE)), hbm, hbm, hbm],
        out_specs=col(D_FQK),
        scratch_shapes=[pltpu.VMEM((2, pg, D_FQK, PAGE_SIZE), F32), pltpu.VMEM((2, pg, D_FQK, PAGE_SIZE), F32),
                        pltpu.VMEM((2, pg, H_F, PAGE_SIZE), F32), pltpu.SemaphoreType.DMA((2, 3)),
                        stat, stat, stat, pltpu.VMEM((D_FQK, LANE), F32), pltpu.VMEM((D_FQK, LANE), F32),
                        pltpu.SMEM((1,), jnp.int32)],
    )
    return pl.pallas_call(
        functools.partial(_fox_decode_body, n_pages=pg, n_tab=n_tab),
        out_shape=jax.ShapeDtypeStruct((ns, D_FQK, 1), F32),
        grid_spec=grid_spec,
        compiler_params=_cparams(("arbitrary",)),
        name="fox_decode",
    )(page_table, n_groups, fq_col, fk_col, fv_col, lfn_col, msuf, jnp.ones((LANE, LANE), BF16), pex,
      ck_t, cv_t, clf_t)


def _outproj_body(x_ref, gt_ref, or_ref, of_ref, gr_ref, gf_ref, wb_ref, wo_ref, o_ref, *, per_row):
    gt = _mod_rows(gt_ref, per_row)
    merged = (gr_ref[0].astype(F32) * _dot(or_ref[0], wb_ref[:D_RV, :])
              + gf_ref[0].astype(F32) * _dot(of_ref[0], wb_ref[D_RV:, :]))
    y = _dot(merged.astype(BF16), wo_ref[...])
    o_ref[0] = x_ref[0] + gt * y


def _outproj(x, mod, per_row, row_off, o_r, o_f, gr, gf, w_branch, w_out):
    nb, nt, _ = x.shape
    tm = min(FFN_TILE, nt)
    row = lambda w: pl.BlockSpec((1, tm, w), lambda b, t: (b, t, 0))
    return pl.pallas_call(
        functools.partial(_outproj_body, per_row=per_row),
        out_shape=jax.ShapeDtypeStruct(x.shape, F32),
        grid=(nb, nt // tm),
        in_specs=[row(D_MODEL), _mod_specs(1, per_row, nt, row_off)[2], row(D_RV), row(D_FQK),
                  row(D_MODEL), row(D_MODEL), _const_spec((D_RV + D_FQK, D_MODEL)),
                  _const_spec((D_MODEL, D_MODEL))],
        out_specs=row(D_MODEL),
        compiler_params=_cparams(("parallel", "parallel")),
        name="outproj",
    )(x, mod, o_r, o_f, gr, gf, w_branch, w_out)


def _rope_tables(pos):
    half = DK_R // 2
    inv = ROPE_BASE ** (-jnp.arange(half, dtype=F32) / half)
    ang = pos.astype(F32)[:, None] * inv[None, :]
    cos, sin = jnp.cos(ang), jnp.sin(ang)
    return jnp.concatenate([cos, cos], axis=1), jnp.concatenate([-sin, sin], axis=1)


def _permute_w_in(w_in):
    sizes = (D_RQK, D_RQK, D_RV, D_RV, D_FQK, D_FQK, D_FQK, H_F, D_MODEL, D_MODEL)
    offs = [0]
    for s in sizes:
        offs.append(offs[-1] + s)
    part = lambda i: w_in[:, offs[i]:offs[i + 1]]
    fl = jnp.pad(part(7), ((0, 0), (0, LANE - H_F)))
    return jnp.concatenate([part(i) for i in (0, 1, 2, 3, 4, 5, 6, 8, 9)] + [fl], axis=1).astype(BF16)


def kernel(x_prompt, x_sample, cache_k, cache_v, cache_logf, state_ret, page_table, c_prompt, c_sample,
           w_ada, b_ada, w_ffa_up, w_ffa_down, w_in, b_forget, q_norm_w, k_norm_w, ret_norm_w,
           w_branch, w_out, w_ffb_up, w_ffb_down):
    nb, nt, _ = x_prompt.shape
    ns = x_sample.shape[0]
    depth = w_ada.shape[0]
    past_len = page_table.shape[1] * PAGE_SIZE
    assert x_sample.shape[1] == 1 and nb <= SUBLANE and ns % SUBLANE == 0 and nt % ATT_BLK == 0

    pad = (-(ns + SUBLANE)) % (2 * SUBLANE)
    c_all = jnp.concatenate([c_sample, c_prompt, jnp.zeros((SUBLANE - nb + pad, D_MODEL), F32)], axis=0)
    cos_p, sin_p = _rope_tables(jnp.arange(nt, dtype=jnp.int32))
    cos_s, sin_s = _rope_tables(jnp.full((ns,), past_len, jnp.int32))
    hd = np.arange(D_FQK) // DH_F
    bd = jnp.asarray((hd[:, None] == hd[None, :]).astype(np.float32) / DH_F, dtype=BF16)

    y_p = x_prompt
    y_s = x_sample.reshape(1, ns, D_MODEL)
    outs = [[] for _ in range(8)]
    for l in range(depth):
        bf = lambda w: w[l].astype(BF16)
        w_up_a, w_dn_a, w_up_b, w_dn_b = bf(w_ffa_up), bf(w_ffa_down), bf(w_ffb_up), bf(w_ffb_down)
        w_br, w_o = bf(w_branch), bf(w_out)
        w_inp = _permute_w_in(w_in[l])
        bf_row = jnp.pad(b_forget[l], (0, LANE - H_F)).reshape(1, LANE)
        qn = jnp.tile(q_norm_w[l], H_F).reshape(1, D_FQK)
        kn = jnp.tile(k_norm_w[l], H_F).reshape(1, D_FQK)
        rn = ret_norm_w[l].reshape(1, D_RV)
        mod = _adaln(c_all, bf(w_ada), b_ada[l])
        qk_bound = 1.02 * DH_F ** 0.5 * jnp.max(jnp.abs(q_norm_w[l])) * jnp.max(jnp.abs(k_norm_w[l]))
        thr = EXP_UNDERFLOW + 2.0 * qk_bound
        page_view = lambda c, w: jnp.transpose(c, (0, 2, 3, 1)).reshape(c.shape[0], w, PAGE_SIZE)

        y_p = _ffn(y_p, mod, 0, False, ns, w_up_a, w_dn_a)
        pj = _inproj(y_p, mod, False, ns, cos_p, sin_p, w_inp, bf_row, qn, kn, bd)
        qa, ka, base, skip = _forget_terms(pj["lfp"], ATT_BLK, thr)
        skip = skip[:, :, 0, :H_F].astype(jnp.int32).reshape(-1)
        o_f = _fox_prompt(skip, pj["fq"], qa, pj["fkb"], ka, pj["fvb"], base, ATT_BLK)
        o_r, st_p = _ret_prompt(pj["rq"], pj["rk"], pj["rv"], pj["rg"], rn)
        y_p = _outproj(y_p, mod, False, ns, o_r, o_f, pj["gr"], pj["gf"], w_br, w_o)
        y_p = _ffn(y_p, mod, 2, False, ns, w_up_b, w_dn_b)

        y_s = _ffn(y_s, mod, 0, True, 0, w_up_a, w_dn_a)
        sj = _inproj(y_s, mod, True, 0, cos_s, sin_s, w_inp, bf_row, qn, kn, bd)
        seq = lambda a: a.reshape(ns, 1, a.shape[-1])
        col = lambda a: jnp.transpose(a, (2, 1, 0))
        as_col = lambda a: a.astype(F32).reshape(ns, a.shape[-1], 1)
        o_f_s = _fox_decode(page_table, as_col(sj["fq"]), as_col(sj["fkb"]), col(sj["fvt"]), col(sj["lft"]),
                            page_view(cache_k[l], D_FQK), page_view(cache_v[l], D_FQK),
                            jnp.transpose(cache_logf[l], (0, 2, 1)), thr)
        o_r_s, st_s = _ret_decode(seq(sj["rq"]), seq(sj["rk"]), seq(sj["rv"]), seq(sj["rg"]), rn, state_ret[l])
        y_s = _outproj(y_s, mod, True, 0, o_r_s.reshape(1, ns, D_RV), o_f_s.reshape(1, ns, D_FQK).astype(BF16),
                       sj["gr"], sj["gf"], w_br, w_o)
        y_s = _ffn(y_s, mod, 2, True, 0, w_up_b, w_dn_b)

        rows = lambda a: jnp.transpose(a.reshape(a.shape[0], H_F, DH_F, a.shape[2]), (0, 3, 1, 2))
        for dst, val in zip(outs, (rows(pj["fkt"]), rows(pj["fvt"]), jnp.transpose(pj["lft"], (0, 2, 1)), st_p,
                                   rows(sj["fkt"]).reshape(ns, 1, H_F, DH_F),
                                   rows(sj["fvt"]).reshape(ns, 1, H_F, DH_F),
                                   jnp.transpose(sj["lft"], (0, 2, 1)).reshape(ns, 1, H_F), st_s)):
            dst.append(val)

    return (y_p, y_s.reshape(ns, 1, D_MODEL), *[jnp.stack(o) for o in outs])
```

```python
import functools
import math

import jax
import jax.numpy as jnp
import numpy as np
from jax import lax
from jax.experimental import pallas as pl
from jax.experimental.pallas import tpu as pltpu

D_MODEL = 1024
PAGE_SIZE = 128
H_R, DK_R, DV_R = 4, 128, 256
RET_CHUNK = 128
ROPE_BASE = 10000.0
H_F, DH_F = 8, 64
D_FF = 2816
EPS = 1e-6
N_SUB = 3
NEG_INF = -1e30
EXP_UNDERFLOW = 110.0

D_RQK = H_R * DK_R
D_RV = H_R * DV_R
D_FQK = H_F * DH_F
N_PAIR = H_F // 2
LANE = 128
SUBLANE = 8

F32 = jnp.float32
BF16 = jnp.bfloat16

_SEG = {}
_off = 0
for _name, _w in (("rq", D_RQK), ("rk", D_RQK), ("rv", D_RV), ("rg", D_RV), ("fq", D_FQK), ("fk", D_FQK),
                  ("fv", D_FQK), ("gr", D_MODEL), ("gf", D_MODEL), ("fl", LANE)):
    _SEG[_name] = (_off, _off + _w)
    _off += _w
D_IN_PAD = _off

ROW_TILE = 512
ATT_BLK = 512
FFN_TILE = 1024
FF_CHUNK = 256
RET_BLOCK = 512
DEC_PAGES = 4
DEC_SEQS = 4
VMEM_LIMIT = 56 * 2 ** 20


def _cparams(sem):
    return pltpu.CompilerParams(dimension_semantics=sem, vmem_limit_bytes=VMEM_LIMIT)


def _const_spec(shape):
    nd = len(shape)
    return pl.BlockSpec(shape, lambda *_: (0,) * nd, pipeline_mode=pl.Buffered(1))


def _dot(a, b):
    return jnp.dot(a, b, preferred_element_type=F32)


def _dot_nt(a, b):
    return lax.dot_general(a, b, (((1,), (1,)), ((), ())), preferred_element_type=F32)


def _dot_tn(a, b):
    return lax.dot_general(a, b, (((0,), (0,)), ((), ())), preferred_element_type=F32)


def _split3(x):
    hi = x.astype(BF16)
    r1 = x - hi.astype(F32)
    mid = r1.astype(BF16)
    lo = (r1 - mid.astype(F32)).astype(BF16)
    return hi, mid, lo


def _dot3_rhs(a_bf16, x):
    hi, mid, lo = _split3(x)
    return _dot(a_bf16, hi) + _dot(a_bf16, mid) + _dot(a_bf16, lo)


def _dot3_lhs(x, b_bf16):
    hi, mid, lo = _split3(x)
    return _dot(hi, b_bf16) + _dot(mid, b_bf16) + _dot(lo, b_bf16)


def _log_sigmoid(x):
    return jnp.minimum(x, 0.0) - jnp.log1p(jnp.exp(-jnp.abs(x)))


def _mod_rows(ref, per_row):
    if per_row:
        return ref[0]
    return ref[0, pl.ds(pl.program_id(0), 1), :]


def _modulated_norm(x, shift, scale):
    ms = jnp.mean(x * x, axis=-1, keepdims=True)
    return (x * lax.rsqrt(ms + EPS)) * (1.0 + scale) + shift


def _mod_specs(sub, per_row, n_rows, row_off):
    specs = []
    for k in range(3):
        j = sub * 3 + k
        if per_row:
            specs.append(pl.BlockSpec((1, n_rows, D_MODEL), lambda b, t, j=j: (j, 0, 0)))
        else:
            specs.append(pl.BlockSpec((1, SUBLANE, D_MODEL), lambda b, t, j=j: (j, row_off // SUBLANE, 0)))
    return specs


def _adaln_body(c_ref, w_ref, b_ref, o_ref):
    c = c_ref[...]
    s = c * jax.nn.sigmoid(c)
    o_ref[0] = _dot(s.astype(BF16), w_ref[...]) + b_ref[...]


def _adaln(c_all, w_ada, b_ada):
    rows = c_all.shape[0]
    n = N_SUB * 3
    return pl.pallas_call(
        _adaln_body,
        out_shape=jax.ShapeDtypeStruct((n, rows, D_MODEL), F32),
        grid=(n,),
        in_specs=[pl.BlockSpec((rows, D_MODEL), lambda j: (0, 0)),
                  pl.BlockSpec((D_MODEL, D_MODEL), lambda j: (0, j)),
                  pl.BlockSpec((1, D_MODEL), lambda j: (0, j))],
        out_specs=pl.BlockSpec((1, rows, D_MODEL), lambda j: (j, 0, 0)),
        compiler_params=_cparams(("parallel",)),
        name="adaln",
    )(c_all, w_ada, b_ada.reshape(1, -1))


def _ffn_body(x_ref, sh_ref, sc_ref, gt_ref, wup_ref, wdn_ref, o_ref, *, per_row):
    x = x_ref[0]
    sh, sc, gt = (_mod_rows(r, per_row) for r in (sh_ref, sc_ref, gt_ref))
    h = _modulated_norm(x, sh, sc).astype(BF16)
    acc = None
    for c in range(D_FF // FF_CHUNK):
        lo = c * FF_CHUNK
        g = _dot(h, wup_ref[:, lo:lo + FF_CHUNK])
        u = _dot(h, wup_ref[:, D_FF + lo:D_FF + lo + FF_CHUNK])
        a = (g * jax.nn.sigmoid(g) * u).astype(BF16)
        part = _dot(a, wdn_ref[lo:lo + FF_CHUNK, :])
        acc = part if acc is None else acc + part
    o_ref[0] = x + (0.5 * gt) * acc


def _ffn(x, mod, sub, per_row, row_off, w_up, w_dn):
    nb, nt, _ = x.shape
    tm = min(FFN_TILE, nt)
    x_spec = pl.BlockSpec((1, tm, D_MODEL), lambda b, t: (b, t, 0))
    return pl.pallas_call(
        functools.partial(_ffn_body, per_row=per_row),
        out_shape=jax.ShapeDtypeStruct(x.shape, F32),
        grid=(nb, nt // tm),
        in_specs=[x_spec, *_mod_specs(sub, per_row, nt, row_off),
                  _const_spec((D_MODEL, 2 * D_FF)), _const_spec((D_FF, D_MODEL))],
        out_specs=x_spec,
        compiler_params=_cparams(("parallel", "parallel")),
        name="ffn",
    )(x, mod, mod, mod, w_up, w_dn)


def _inproj_body(*refs, per_row, fused):
    (x_ref, sh_ref, sc_ref, cos_ref, sin_ref, w_ref, bf_ref, qn_ref, kn_ref, bd_ref), refs = refs[:10], refs[10:]
    if fused:
        (rn_ref, ltri_ref, ones_ref, pq_ref, pk_ref, qrow_ref, krow_ref, thr_ref), refs = refs[:8], refs[8:]
        (or_o, st_o, qa_o, ka_o, base_o, skip_o), refs = refs[:6], refs[6:]
    else:
        (rq_o, rk_o, rv_o, rg_o), refs = refs[:4], refs[4:]
    (fq_o, fkb_o, fvb_o, fkt_o, fvt_o, gr_o, gf_o, lft_o), refs = refs[:8], refs[8:]

    x = x_ref[0]
    sh, sc = (_mod_rows(r, per_row) for r in (sh_ref, sc_ref))
    h = _modulated_norm(x, sh, sc).astype(BF16)

    def seg(name):
        lo, hi = _SEG[name]
        return _dot(h, w_ref[:, lo:hi])

    cos, sin = cos_ref[...], sin_ref[...]

    def rope_heads(z, scale):
        heads = []
        for hd in range(H_R):
            zh = z[:, hd * DK_R:(hd + 1) * DK_R]
            r = zh * cos + pltpu.roll(zh, DK_R // 2, 1) * sin
            heads.append((r * scale if scale != 1.0 else r).astype(BF16))
        return heads

    q_heads = rope_heads(seg("rq"), 1.0)
    k_heads = rope_heads(seg("rk"), DK_R ** -0.5)
    rv = seg("rv").astype(BF16)
    rg = seg("rg")
    if fused:
        s_scr, carry, bases = refs
        _retention_block(q_heads, k_heads, rv, rg, rn_ref, or_o, st_o, s_scr)
    else:
        for hd in range(H_R):
            rq_o[0, :, hd * DK_R:(hd + 1) * DK_R] = q_heads[hd]
            rk_o[0, :, hd * DK_R:(hd + 1) * DK_R] = k_heads[hd]
        rv_o[0] = rv
        rg_o[0] = rg.astype(BF16)

    bd = bd_ref[...]

    def head_rms(z, w):
        z2 = z * z
        hi = z2.astype(BF16)
        lo = (z2 - hi.astype(F32)).astype(BF16)
        ms = _dot(hi, bd) + _dot(lo, bd)
        return (z * lax.rsqrt(ms + EPS)) * w

    fq_o[0] = (head_rms(seg("fq"), qn_ref[...]) * (DH_F ** -0.5)).astype(BF16)
    fk = head_rms(seg("fk"), kn_ref[...])
    fkt_o[0] = fk.T
    fkb_o[0] = fk.astype(BF16)
    fv = seg("fv")
    fvt_o[0] = fv.T
    fvb_o[0] = fv.astype(BF16)
    gr_o[0] = jax.nn.sigmoid(seg("gr")).astype(BF16)
    gf_o[0] = jax.nn.sigmoid(seg("gf")).astype(BF16)

    logf = _log_sigmoid(seg("fl") + bf_ref[...])
    lane = lax.broadcasted_iota(jnp.int32, logf.shape, 1)
    logf = jnp.where(lane < H_F, logf, 0.0)
    lft_o[0] = logf.T[:H_F]
    if fused:
        _forget_block(logf, ltri_ref, ones_ref, pq_ref, pk_ref, qrow_ref, krow_ref, thr_ref,
                      qa_o, ka_o, base_o, skip_o, carry, bases)


def _inproj(x, mod, per_row, row_off, cos_t, sin_t, w_in, bf_row, qn, kn, bd, fused=None):
    nb, nt, _ = x.shape
    tm = min(ROW_TILE, nt)
    nblk = nt // tm
    w_aug = N_PAIR * LANE
    row = lambda w: pl.BlockSpec((1, tm, w), lambda b, t: (b, t, 0))
    col = lambda w: pl.BlockSpec((1, w, tm), lambda b, t: (b, 0, t))
    rows = lambda name, w: (name, jax.ShapeDtypeStruct((nb, nt, w), BF16), row(w))
    cols = lambda name, w: (name, jax.ShapeDtypeStruct((nb, w, nt), F32), col(w))
    stat = lambda name: (name, jax.ShapeDtypeStruct((nb, nblk, SUBLANE, LANE), F32),
                         pl.BlockSpec((1, 1, SUBLANE, LANE), lambda b, t: (b, t, 0, 0)))
    ins = [x, mod, mod, cos_t, sin_t, w_in, bf_row, qn, kn, bd]
    in_specs = [row(D_MODEL), *_mod_specs(1, per_row, nt, row_off)[:2],
                pl.BlockSpec((tm, DK_R), lambda b, t: (t, 0)), pl.BlockSpec((tm, DK_R), lambda b, t: (t, 0)),
                _const_spec((D_MODEL, D_IN_PAD)), _const_spec((1, LANE)), _const_spec((1, D_FQK)),
                _const_spec((1, D_FQK)), _const_spec((D_FQK, D_FQK))]
    scratch = []
    if fused is not None:
        assert tm == ATT_BLK and tm % RET_CHUNK == 0
        norm_w, thr = fused
        tables = _forget_tables()
        ins += [norm_w, *tables, jnp.broadcast_to(thr.astype(F32), (1, LANE))]
        in_specs += [_const_spec(a.shape) for a in ins[10:]]
        outs = [rows("o_r", D_RV),
                ("st", jax.ShapeDtypeStruct((nb, H_R, DK_R, DV_R), F32),
                 pl.BlockSpec((1, H_R, DK_R, DV_R), lambda b, t: (b, 0, 0, 0))),
                rows("qa", w_aug), rows("ka", w_aug), stat("base"), stat("skip")]
        scratch = [pltpu.VMEM((H_R, DK_R, DV_R), F32), pltpu.VMEM((1, LANE), F32),
                   pltpu.VMEM((-(-2 * nblk // SUBLANE) * SUBLANE, LANE), F32)]
    else:
        outs = [rows("rq", D_RQK), rows("rk", D_RQK), rows("rv", D_RV), rows("rg", D_RV)]
    outs += [rows("fq", D_FQK), rows("fkb", D_FQK), rows("fvb", D_FQK), cols("fkt", D_FQK), cols("fvt", D_FQK),
             rows("gr", D_MODEL), rows("gf", D_MODEL), cols("lft", H_F)]
    res = pl.pallas_call(
        functools.partial(_inproj_body, per_row=per_row, fused=fused is not None),
        out_shape=[o[1] for o in outs],
        grid=(nb, nblk),
        in_specs=in_specs,
        out_specs=[o[2] for o in outs],
        scratch_shapes=scratch,
        compiler_params=_cparams(("parallel", "arbitrary" if fused is not None else "parallel")),
        name="inproj",
    )(*ins)
    return {o[0]: r for o, r in zip(outs, res)}


def _forget_block(logf, ltri_ref, ones_ref, pq_ref, pk_ref, qrow_ref, krow_ref, thr_ref,
                  qa_o, ka_o, base_o, skip_o, carry, bases):
    i = pl.program_id(1)

    @pl.when(i == 0)
    def _():
        carry[...] = jnp.zeros_like(carry)
        bases[...] = jnp.zeros_like(bases)

    ltri = ltri_ref[...]
    run = jnp.zeros((1, LANE), F32)
    for r in range(logf.shape[0] // LANE):
        rows = slice(r * LANE, (r + 1) * LANE)
        c = _dot3_rhs(ltri, logf[rows]) + run
        run = c[LANE - 1:LANE, :]
        if (r + 1) * LANE * 2 == logf.shape[0]:
            half = run
        hi, mid, lo = _split3(c)
        qa = _dot(hi, pq_ref[0]) + _dot(mid, pq_ref[1]) + _dot(lo, pq_ref[2]) + qrow_ref[...]
        ka = _dot(hi, pk_ref[0]) + _dot(mid, pk_ref[1]) + _dot(lo, pk_ref[2]) + krow_ref[...]
        qa_o[0, rows, :] = qa.astype(BF16)
        ka_o[0, rows, :] = ka.astype(BF16)

    base = carry[...]
    rr = lax.broadcasted_iota(jnp.int32, (2 * SUBLANE, LANE), 0)
    cc = lax.broadcasted_iota(jnp.int32, (2 * SUBLANE, LANE), 1)
    diag = jnp.where(rr == cc, jnp.broadcast_to(base, (2 * SUBLANE, LANE)), 0.0)
    base_o[0, 0] = _dot3_lhs(diag, ones_ref[...])[:SUBLANE]
    carry[...] = base + run

    uu = lax.broadcasted_iota(jnp.int32, bases.shape, 0)
    far = (uu < 2 * i) & (bases[...] - base > thr_ref[...])
    skip_o[0, 0] = jnp.broadcast_to(jnp.sum(jnp.where(far, 1.0, 0.0), axis=0, keepdims=True), (SUBLANE, LANE))
    bases[pl.ds(2 * i, 1), :] = base + half
    bases[pl.ds(2 * i + 1, 1), :] = base + run


def _forget_tables():
    w = N_PAIR * LANE
    pq, pk = np.zeros((3, LANE, w), np.float32), np.zeros((3, LANE, w), np.float32)
    qrow, krow = np.zeros((1, w), np.float32), np.zeros((1, w), np.float32)
    for h in range(H_F):
        l0 = (h // 2) * LANE + (h % 2) * DH_F
        for s in range(3):
            pq[s, h, l0 + s] = 1.0
            pk[s, h, l0 + 3 + s] = -1.0
            qrow[0, l0 + 3 + s] = 1.0
            krow[0, l0 + s] = 1.0
    r = np.arange(LANE)
    ltri = (r[None, :] <= r[:, None]).astype(np.float32)
    as_bf16 = lambda a: jnp.asarray(a, dtype=BF16)
    return (as_bf16(ltri), jnp.ones((LANE, LANE), BF16), as_bf16(pq), as_bf16(pk), jnp.asarray(qrow),
            jnp.asarray(krow))


def _fox_prompt_body(skip_ref, q_ref, qa_ref, k_ref, ka_ref, v_ref, base_ref, o_ref, m_scr, l_scr, acc_scr):
    b = pl.program_id(0)
    hp = pl.program_id(1)
    i = pl.program_id(2)
    tq = q_ref.shape[1]
    tk = tq // 2
    lane2 = lax.broadcasted_iota(jnp.int32, (1, 2 * LANE), 1)
    first = (lane2 % LANE) < DH_F
    qf = jnp.concatenate([q_ref[0], qa_ref[0]], axis=1)
    zero = jnp.zeros_like(qf)
    q_heads = (jnp.where(first, qf, zero), jnp.where(first, zero, qf))

    m_scr[...] = jnp.full(m_scr.shape, NEG_INF, F32)
    l_scr[...] = jnp.zeros_like(l_scr)
    acc_scr[...] = jnp.zeros_like(acc_scr)

    def step(hh, u, r0, causal):
        rows = slice(r0, tq)
        off = pl.multiple_of(u * tk, tk)
        kf = jnp.concatenate([k_ref[0, pl.ds(off, tk), :], ka_ref[0, pl.ds(off, tk), :]], axis=1)
        s = _dot_nt(q_heads[hh][rows], kf)
        if causal:
            row = lax.broadcasted_iota(jnp.int32, s.shape, 0)
            col = lax.broadcasted_iota(jnp.int32, s.shape, 1)
            s = jnp.where(col <= row, s, NEG_INF)
        h = 2 * hp + hh
        d = base_ref[0, i, pl.ds(h, 1), :] - base_ref[0, u // 2, pl.ds(h, 1), :]
        m_prev = m_scr[hh, rows]
        m_next = jnp.maximum(m_prev, jnp.max(s, axis=1, keepdims=True) + d)
        alpha = jnp.exp(m_prev - m_next)
        sub = m_next - d
        p = jnp.exp(s - jnp.concatenate([sub] * (tk // LANE), axis=1))
        l_scr[hh, rows] = alpha * l_scr[hh, rows] + jnp.sum(p, axis=1, keepdims=True)
        acc_scr[hh, rows] = alpha * acc_scr[hh, rows] + _dot(p.astype(BF16), v_ref[0, pl.ds(off, tk), :])
        m_scr[hh, rows] = m_next

    for hh in range(2):
        u0 = skip_ref[(b * pl.num_programs(2) + i) * H_F + 2 * hp + hh]

        def full_step(u, carry, hh=hh):
            step(hh, u, 0, False)
            return carry

        lax.fori_loop(u0, 2 * i, full_step, 0)
        step(hh, 2 * i, 0, True)
        step(hh, 2 * i + 1, tk, True)

    lane = lax.broadcasted_iota(jnp.int32, (1, LANE), 1)
    o = jnp.where(lane < DH_F, acc_scr[0] / l_scr[0], acc_scr[1] / l_scr[1])
    o_ref[0] = o.astype(o_ref.dtype)


def _fox_prompt(skip, fq, qa, fkb, ka, fvb, base, blk):
    nb, nt, _ = fq.shape
    nblk = nt // blk
    qspec = pl.BlockSpec((1, blk, LANE), lambda b, p, i, sk: (b, i, p))
    kspec = pl.BlockSpec((1, nt, LANE), lambda b, p, i, sk: (b, 0, p))
    grid_spec = pltpu.PrefetchScalarGridSpec(
        num_scalar_prefetch=1,
        grid=(nb, N_PAIR, nblk),
        in_specs=[qspec, qspec, kspec, kspec, kspec,
                  pl.BlockSpec((1, nblk, SUBLANE, LANE), lambda b, p, i, sk: (b, 0, 0, 0))],
        out_specs=qspec,
        scratch_shapes=[pltpu.VMEM((2, blk, LANE), F32)] * 3,
    )
    return pl.pallas_call(
        _fox_prompt_body,
        out_shape=jax.ShapeDtypeStruct((nb, nt, D_FQK), BF16),
        grid_spec=grid_spec,
        compiler_params=_cparams(("parallel", "parallel", "parallel")),
        name="fox_prompt",
    )(skip, fq, qa, fkb, ka, fvb, base)


def _log_gamma(h):
    return math.log(1.0 - 2.0 ** (-5.0 - h))


def _group_norm_gate(o, w, rg):
    mu = jnp.mean(o, axis=-1, keepdims=True)
    var = jnp.mean(jnp.square(o - mu), axis=-1, keepdims=True)
    on = (o - mu) * lax.rsqrt(var + EPS)
    rg = rg.astype(F32)
    return (rg * jax.nn.sigmoid(rg)) * (on * w)


def _retention_block(q_heads, k_heads, v_all, g_all, w_ref, o_ref, st_ref, s_scr):
    t = pl.program_id(1)

    @pl.when(t == 0)
    def _():
        s_scr[...] = jnp.zeros_like(s_scr)

    L = RET_CHUNK
    ri = lax.broadcasted_iota(jnp.int32, (L, L), 0).astype(F32)
    ci = lax.broadcasted_iota(jnp.int32, (L, L), 1).astype(F32)
    diff = ri - ci
    rv = lax.broadcasted_iota(jnp.int32, (L, DV_R), 0).astype(F32)
    for c in range(v_all.shape[0] // L):
        rows = slice(c * L, (c + 1) * L)
        for h in range(H_R):
            lg = _log_gamma(h)
            vc = slice(h * DV_R, (h + 1) * DV_R)
            q = q_heads[h][rows]
            k = k_heads[h][rows]
            v = v_all[rows, vc]
            decay = jnp.where(diff >= 0, jnp.exp(lg * jnp.maximum(diff, 0.0)), 0.0)
            scores = _dot_nt(q, k) * decay
            s0 = s_scr[h]
            o = _dot(scores.astype(BF16), v) + _dot(q, s0.astype(BF16)) * jnp.exp(lg * (rv + 1.0))
            kd = (k.astype(F32) * jnp.exp(lg * (L - 1.0 - ri))).astype(BF16)
            s_scr[h] = s0 * math.exp(lg * L) + _dot_tn(kd, v)
            o_ref[0, rows, vc] = _group_norm_gate(o, w_ref[:, vc], g_all[rows, vc]).astype(o_ref.dtype)

    @pl.when(t == pl.num_programs(1) - 1)
    def _():
        st_ref[0] = s_scr[...]


def _ret_decode_body(q_ref, k_ref, v_ref, g_ref, w_ref, s_ref, o_ref, so_ref):
    rr = lax.broadcasted_iota(jnp.int32, (DK_R, DK_R), 0)
    cc = lax.broadcasted_iota(jnp.int32, (DK_R, DK_R), 1)
    eye = rr == cc
    for n in range(q_ref.shape[0]):
        for h in range(H_R):
            gamma = math.exp(_log_gamma(h))
            kc = slice(h * DK_R, (h + 1) * DK_R)
            vc = slice(h * DV_R, (h + 1) * DV_R)
            q = q_ref[n, :, kc]
            k = k_ref[n, :, kc]
            v = v_ref[n, :, vc]
            s0 = s_ref[n, h]
            qk = jnp.sum(q.astype(F32) * k.astype(F32), axis=-1, keepdims=True)
            q_s = _dot(jnp.broadcast_to(q, (2 * SUBLANE, DK_R)), s0.astype(BF16))[:1]
            o = qk.astype(BF16).astype(F32) * v.astype(F32) + q_s * gamma
            k_diag = jnp.where(eye, jnp.broadcast_to(k.astype(F32), (DK_R, DK_R)), 0.0).astype(BF16)
            v_rows = jnp.broadcast_to(v.astype(F32), (DK_R, DV_R)).astype(BF16)
            so_ref[n, h] = s0 * gamma + _dot(k_diag, v_rows)
            o_ref[n, :, vc] = _group_norm_gate(o, w_ref[:, vc], g_ref[n, :, vc]).astype(o_ref.dtype)


def _ret_decode(rq, rk, rv, rg, norm_w, state):
    ns = rq.shape[0]
    sb = min(DEC_SEQS, ns)
    spec = lambda w: pl.BlockSpec((sb, 1, w), lambda i: (i, 0, 0))
    st_spec = pl.BlockSpec((sb, H_R, DK_R, DV_R), lambda i: (i, 0, 0, 0))
    return pl.pallas_call(
        _ret_decode_body,
        out_shape=[jax.ShapeDtypeStruct((ns, 1, D_RV), BF16), jax.ShapeDtypeStruct(state.shape, F32)],
        grid=(ns // sb,),
        in_specs=[spec(D_RQK), spec(D_RQK), spec(D_RV), spec(D_RV), _const_spec((1, D_RV)), st_spec],
        out_specs=[spec(D_RV), st_spec],
        compiler_params=_cparams(("parallel",)),
        name="ret_decode",
    )(rq, rk, rv, rg, norm_w, state)


def _page_totals_body(lf_ref, ones_ref, o_ref):
    ones = ones_ref[...]
    for c in range(o_ref.shape[0]):
        hi, mid, lo = _split3(lf_ref[c * LANE:(c + 1) * LANE, :])
        o_ref[c:c + 1, :] = (_dot_nt(ones, hi) + _dot_nt(ones, mid) + _dot_nt(ones, lo))[:1]


def _page_totals(clf_t):
    n_phys = clf_t.shape[0]
    rows = clf_t.reshape(n_phys * H_F, PAGE_SIZE)
    blk = SUBLANE * LANE
    n_rows = -(-rows.shape[0] // blk) * blk
    if n_rows != rows.shape[0]:
        rows = jnp.pad(rows, ((0, n_rows - rows.shape[0]), (0, 0)))
    step = max(d for d in range(blk, 8 * blk + 1, blk) if n_rows % d == 0)
    tot = pl.pallas_call(
        _page_totals_body,
        out_shape=jax.ShapeDtypeStruct((n_rows // LANE, LANE), F32),
        grid=(n_rows // step,),
        in_specs=[pl.BlockSpec((step, PAGE_SIZE), lambda i: (i, 0)), _const_spec((2 * SUBLANE, LANE))],
        out_specs=pl.BlockSpec((step // LANE, LANE), lambda i: (i, 0)),
        compiler_params=_cparams(("parallel",)),
        name="page_totals",
    )(rows, jnp.ones((2 * SUBLANE, LANE), BF16))
    return tot.reshape(-1)[:n_phys * H_F].reshape(n_phys, H_F)


def _fox_decode_body(pt_ref, ng_ref, q_ref, kn_ref, vn_ref, lfn_ref, msuf_ref, ones_ref, pex_ref,
                     ck_hbm, cv_hbm, clf_hbm, o_ref,
                     kbuf, vbuf, lfbuf, sem, m_run, l_run, tail, acc_run, qrep, count, *, n_pages, n_tab):
    s = pl.program_id(0)
    ng = ng_ref[s]
    stat = (SUBLANE, LANE)

    def page_copies(seq, group, slot):
        copies = []
        for r in range(n_pages):
            page = pt_ref[seq, n_tab - 1 - (group * n_pages + r)]
            for kind, (hbm, buf) in enumerate(((ck_hbm, kbuf), (cv_hbm, vbuf), (clf_hbm, lfbuf))):
                copies.append(pltpu.make_async_copy(hbm.at[page], buf.at[slot, r], sem.at[slot, kind]))
        return copies

    def head_sums(x):
        return x.reshape(H_F, DH_F // SUBLANE, SUBLANE, LANE).sum(axis=1).sum(axis=1)

    @pl.when(s == 0)
    def _():
        count[0] = 0
        for c in page_copies(0, 0, 0):
            c.start()

    first = count[0]
    q = jnp.broadcast_to(q_ref[0], (D_FQK, LANE))
    qrep[...] = q
    s_self = head_sums(q * kn_ref[0])
    m_run[...] = s_self
    l_run[...] = jnp.zeros(stat, F32)
    acc_run[...] = jnp.zeros_like(acc_run)
    tail[...] = jnp.broadcast_to(lfn_ref[0], stat)

    def group_step(g, carry):
        slot = lax.rem(first + g, 2)

        @pl.when(g + 1 < ng)
        def _():
            for c in page_copies(s, g + 1, 1 - slot):
                c.start()

        @pl.when(jnp.logical_and(g + 1 == ng, s + 1 < pl.num_programs(0)))
        def _():
            for c in page_copies(s + 1, 0, 1 - slot):
                c.start()

        for c in page_copies(s, g, slot):
            c.wait()

        qv = qrep[...]
        rows = [lfbuf[slot, r] for r in range(n_pages)]
        rows.append(jnp.zeros((LANE - n_pages * SUBLANE, LANE), F32))
        lf = jnp.concatenate(rows, axis=0)
        suf = _dot3_lhs(lf, msuf_ref[...])
        tot = _dot3_lhs(lf, ones_ref[...])
        pre = _dot3_rhs(pex_ref[...], tot)
        t_in = tail[...]
        last = slice((n_pages - 1) * SUBLANE, n_pages * SUBLANE)
        tail[...] = t_in + pre[last] + tot[last]
        s_pages = []
        m_step = jnp.full((SUBLANE, 1), NEG_INF, F32)
        for r in range(n_pages):
            pr = slice(r * SUBLANE, (r + 1) * SUBLANE)
            sr = head_sums(kbuf[slot, r] * qv) + (suf[pr] + pre[pr] + t_in)
            s_pages.append(sr)
            m_step = jnp.maximum(m_step, jnp.max(sr, axis=1, keepdims=True))
        m_prev = m_run[...]
        m_new = jnp.maximum(m_prev, m_step)
        alpha = jnp.exp(m_prev - m_new)
        p_pages = [jnp.exp(sr - m_new) for sr in s_pages]
        l_step = jnp.zeros((SUBLANE, 1), F32)
        for p in p_pages:
            l_step = l_step + jnp.sum(p, axis=1, keepdims=True)
        for h in range(H_F):
            hr = slice(h * DH_F, (h + 1) * DH_F)
            acc = acc_run[hr, :] * alpha[h:h + 1, :]
            for r in range(n_pages):
                acc = acc + vbuf[slot, r, hr, :] * p_pages[r][h:h + 1, :]
            acc_run[hr, :] = acc
        l_run[...] = alpha * l_run[...] + l_step
        m_run[...] = m_new
        return carry

    lax.fori_loop(0, ng, group_step, 0)
    count[0] = first + ng

    w_self = jnp.exp(s_self - m_run[...])
    l_tot = l_run[...] + w_self
    for h in range(H_F):
        hr = slice(h * DH_F, (h + 1) * DH_F)
        acc_col = jnp.sum(acc_run[hr, :], axis=1, keepdims=True)
        o_ref[0, hr, :] = (acc_col + w_self[h:h + 1, :1] * vn_ref[0, hr, :]) / l_tot[h:h + 1, :1]


def _fox_decode(page_table, fq_col, fk_col, fv_col, lfn_col, ck_t, cv_t, clf_t, thr):
    ns, n_tab = page_table.shape
    pg = min(DEC_PAGES, n_tab)
    assert pg * SUBLANE <= LANE and n_tab % pg == 0

    tot = _page_totals(clf_t)[page_table]
    after = lfn_col.reshape(ns, 1, H_F) + jnp.cumsum(tot[:, ::-1], axis=1)[:, ::-1] - tot
    need = jnp.maximum(jnp.sum(jnp.any(after >= -thr, axis=-1), axis=1), 1)
    n_groups = ((need + pg - 1) // pg).astype(jnp.int32)

    r = np.arange(LANE)
    msuf = jnp.asarray((r[:, None] > r[None, :]).astype(np.float32), dtype=BF16)
    same_head = (r[:, None] % SUBLANE) == (r[None, :] % SUBLANE)
    in_step = (r[:, None] < pg * SUBLANE) & (r[None, :] < pg * SUBLANE)
    pex = jnp.asarray((same_head & in_step & (r[None, :] // SUBLANE < r[:, None] // SUBLANE)).astype(np.float32),
                      dtype=BF16)
    col = lambda w: pl.BlockSpec((1, w, 1), lambda s, pt, ng: (s, 0, 0))
    hbm = pl.BlockSpec(memory_space=pl.ANY)
    stat = pltpu.VMEM((SUBLANE, LANE), F32)
    grid_spec = pltpu.PrefetchScalarGridSpec(
        num_scalar_prefetch=2,
        grid=(ns,),
        in_specs=[col(D_FQK), col(D_FQK), col(D_FQK), col(H_F), _const_spec((LANE, LANE)),
                  _const_spec((LANE, LANE)), _const_spec((LANE, LANE)), hbm, hbm, hbm],
        out_specs=col(D_FQK),
        scratch_shapes=[pltpu.VMEM((2, pg, D_FQK, PAGE_SIZE), F32), pltpu.VMEM((2, pg, D_FQK, PAGE_SIZE), F32),
                        pltpu.VMEM((2, pg, H_F, PAGE_SIZE), F32), pltpu.SemaphoreType.DMA((2, 3)),
                        stat, stat, stat, pltpu.VMEM((D_FQK, LANE), F32), pltpu.VMEM((D_FQK, LANE), F32),
                        pltpu.SMEM((1,), jnp.int32)],
    )
    return pl.pallas_call(
        functools.partial(_fox_decode_body, n_pages=pg, n_tab=n_tab),
        out_shape=jax.ShapeDtypeStruct((ns, D_FQK, 1), F32),
        grid_spec=grid_spec,
        compiler_params=_cparams(("arbitrary",)),
        name="fox_decode",
    )(page_table, n_groups, fq_col, fk_col, fv_col, lfn_col, msuf, jnp.ones((LANE, LANE), BF16), pex,
      ck_t, cv_t, clf_t)


def _outproj_body(x_ref, gt_ref, or_ref, of_ref, gr_ref, gf_ref, wb_ref, wo_ref, o_ref, *, per_row):
    gt = _mod_rows(gt_ref, per_row)
    merged = (gr_ref[0].astype(F32) * _dot(or_ref[0], wb_ref[:D_RV, :])
              + gf_ref[0].astype(F32) * _dot(of_ref[0], wb_ref[D_RV:, :]))
    y = _dot(merged.astype(BF16), wo_ref[...])
    o_ref[0] = x_ref[0] + gt * y


def _outproj(x, mod, per_row, row_off, o_r, o_f, gr, gf, w_branch, w_out):
    nb, nt, _ = x.shape
    tm = min(FFN_TILE, nt)
    row = lambda w: pl.BlockSpec((1, tm, w), lambda b, t: (b, t, 0))
    return pl.pallas_call(
        functools.partial(_outproj_body, per_row=per_row),
        out_shape=jax.ShapeDtypeStruct(x.shape, F32),
        grid=(nb, nt // tm),
        in_specs=[row(D_MODEL), _mod_specs(1, per_row, nt, row_off)[2], row(D_RV), row(D_FQK),
                  row(D_MODEL), row(D_MODEL), _const_spec((D_RV + D_FQK, D_MODEL)),
                  _const_spec((D_MODEL, D_MODEL))],
        out_specs=row(D_MODEL),
        compiler_params=_cparams(("parallel", "parallel")),
        name="outproj",
    )(x, mod, o_r, o_f, gr, gf, w_branch, w_out)


def _rope_tables(pos):
    half = DK_R // 2
    inv = ROPE_BASE ** (-jnp.arange(half, dtype=F32) / half)
    ang = pos.astype(F32)[:, None] * inv[None, :]
    cos, sin = jnp.cos(ang), jnp.sin(ang)
    return jnp.concatenate([cos, cos], axis=1), jnp.concatenate([-sin, sin], axis=1)


def _permute_w_in(w_in):
    sizes = (D_RQK, D_RQK, D_RV, D_RV, D_FQK, D_FQK, D_FQK, H_F, D_MODEL, D_MODEL)
    offs = [0]
    for s in sizes:
        offs.append(offs[-1] + s)
    part = lambda i: w_in[:, offs[i]:offs[i + 1]]
    fl = jnp.pad(part(7), ((0, 0), (0, LANE - H_F)))
    return jnp.concatenate([part(i) for i in (0, 1, 2, 3, 4, 5, 6, 8, 9)] + [fl], axis=1).astype(BF16)


def kernel(x_prompt, x_sample, cache_k, cache_v, cache_logf, state_ret, page_table, c_prompt, c_sample,
           w_ada, b_ada, w_ffa_up, w_ffa_down, w_in, b_forget, q_norm_w, k_norm_w, ret_norm_w,
           w_branch, w_out, w_ffb_up, w_ffb_down):
    nb, nt, _ = x_prompt.shape
    ns = x_sample.shape[0]
    depth = w_ada.shape[0]
    past_len = page_table.shape[1] * PAGE_SIZE
    assert x_sample.shape[1] == 1 and nb <= SUBLANE and ns % SUBLANE == 0 and nt % ATT_BLK == 0

    pad = (-(ns + SUBLANE)) % (2 * SUBLANE)
    c_all = jnp.concatenate([c_sample, c_prompt, jnp.zeros((SUBLANE - nb + pad, D_MODEL), F32)], axis=0)
    cos_p, sin_p = _rope_tables(jnp.arange(nt, dtype=jnp.int32))
    cos_s, sin_s = _rope_tables(jnp.full((ns,), past_len, jnp.int32))
    hd = np.arange(D_FQK) // DH_F
    bd = jnp.asarray((hd[:, None] == hd[None, :]).astype(np.float32) / DH_F, dtype=BF16)

    y_p = x_prompt
    y_s = x_sample.reshape(1, ns, D_MODEL)
    outs = [[] for _ in range(8)]
    for l in range(depth):
        bf = lambda w: w[l].astype(BF16)
        w_up_a, w_dn_a, w_up_b, w_dn_b = bf(w_ffa_up), bf(w_ffa_down), bf(w_ffb_up), bf(w_ffb_down)
        w_br, w_o = bf(w_branch), bf(w_out)
        w_inp = _permute_w_in(w_in[l])
        bf_row = jnp.pad(b_forget[l], (0, LANE - H_F)).reshape(1, LANE)
        qn = jnp.tile(q_norm_w[l], H_F).reshape(1, D_FQK)
        kn = jnp.tile(k_norm_w[l], H_F).reshape(1, D_FQK)
        rn = ret_norm_w[l].reshape(1, D_RV)
        mod = _adaln(c_all, bf(w_ada), b_ada[l])
        qk_bound = 1.02 * DH_F ** 0.5 * jnp.max(jnp.abs(q_norm_w[l])) * jnp.max(jnp.abs(k_norm_w[l]))
        thr = EXP_UNDERFLOW + 2.0 * qk_bound
        page_view = lambda c, w: jnp.transpose(c, (0, 2, 3, 1)).reshape(c.shape[0], w, PAGE_SIZE)

        y_p = _ffn(y_p, mod, 0, False, ns, w_up_a, w_dn_a)
        pj = _inproj(y_p, mod, False, ns, cos_p, sin_p, w_inp, bf_row, qn, kn, bd, fused=(rn, thr))
        skip = pj["skip"][:, :, 0, :H_F].astype(jnp.int32).reshape(-1)
        o_f = _fox_prompt(skip, pj["fq"], pj["qa"], pj["fkb"], pj["ka"], pj["fvb"], pj["base"], ATT_BLK)
        st_p = pj["st"]
        y_p = _outproj(y_p, mod, False, ns, pj["o_r"], o_f, pj["gr"], pj["gf"], w_br, w_o)
        y_p = _ffn(y_p, mod, 2, False, ns, w_up_b, w_dn_b)

        y_s = _ffn(y_s, mod, 0, True, 0, w_up_a, w_dn_a)
        sj = _inproj(y_s, mod, True, 0, cos_s, sin_s, w_inp, bf_row, qn, kn, bd)
        seq = lambda a: a.reshape(ns, 1, a.shape[-1])
        col = lambda a: jnp.transpose(a, (2, 1, 0))
        as_col = lambda a: a.astype(F32).reshape(ns, a.shape[-1], 1)
        o_f_s = _fox_decode(page_table, as_col(sj["fq"]), as_col(sj["fkb"]), col(sj["fvt"]), col(sj["lft"]),
                            page_view(cache_k[l], D_FQK), page_view(cache_v[l], D_FQK),
                            jnp.transpose(cache_logf[l], (0, 2, 1)), thr)
        o_r_s, st_s = _ret_decode(seq(sj["rq"]), seq(sj["rk"]), seq(sj["rv"]), seq(sj["rg"]), rn, state_ret[l])
        y_s = _outproj(y_s, mod, True, 0, o_r_s.reshape(1, ns, D_RV), o_f_s.reshape(1, ns, D_FQK).astype(BF16),
                       sj["gr"], sj["gf"], w_br, w_o)
        y_s = _ffn(y_s, mod, 2, True, 0, w_up_b, w_dn_b)

        rows = lambda a: jnp.transpose(a.reshape(a.shape[0], H_F, DH_F, a.shape[2]), (0, 3, 1, 2))
        for dst, val in zip(outs, (rows(pj["fkt"]), rows(pj["fvt"]), jnp.transpose(pj["lft"], (0, 2, 1)), st_p,
                                   rows(sj["fkt"]).reshape(ns, 1, H_F, DH_F),
                                   rows(sj["fvt"]).reshape(ns, 1, H_F, DH_F),
                                   jnp.transpose(sj["lft"], (0, 2, 1)).reshape(ns, 1, H_F), st_s)):
            dst.append(val)

    return (y_p, y_s.reshape(ns, 1, D_MODEL), *[jnp.stack(o) for o in outs])
```

```python
import functools
import math

import jax
import jax.numpy as jnp
import numpy as np
from jax import lax
from jax.experimental import pallas as pl
from jax.experimental.pallas import tpu as pltpu

D_MODEL = 1024
PAGE_SIZE = 128
H_R, DK_R, DV_R = 4, 128, 256
RET_CHUNK = 128
ROPE_BASE = 10000.0
H_F, DH_F = 8, 64
D_FF = 2816
EPS = 1e-6
N_SUB = 3
NEG_INF = -1e30
EXP_UNDERFLOW = 110.0

D_RQK = H_R * DK_R
D_RV = H_R * DV_R
D_FQK = H_F * DH_F
N_PAIR = H_F // 2
LANE = 128
SUBLANE = 8

F32 = jnp.float32
BF16 = jnp.bfloat16

_SEG = {}
_off = 0
for _name, _w in (("rq", D_RQK), ("rk", D_RQK), ("rv", D_RV), ("rg", D_RV), ("fq", D_FQK), ("fk", D_FQK),
                  ("fv", D_FQK), ("gr", D_MODEL), ("gf", D_MODEL), ("fl", LANE)):
    _SEG[_name] = (_off, _off + _w)
    _off += _w
D_IN_PAD = _off

ROW_TILE = 512
ATT_BLK = 512
FFN_TILE = 1024
FF_CHUNK = 256
RET_BLOCK = 512
DEC_PAGES = 4
DEC_SEQS = 4
VMEM_LIMIT = 56 * 2 ** 20


def _cparams(sem):
    return pltpu.CompilerParams(dimension_semantics=sem, vmem_limit_bytes=VMEM_LIMIT)


def _const_spec(shape):
    nd = len(shape)
    return pl.BlockSpec(shape, lambda *_: (0,) * nd, pipeline_mode=pl.Buffered(1))


def _dot(a, b):
    return jnp.dot(a, b, preferred_element_type=F32)


def _dot_nt(a, b):
    return lax.dot_general(a, b, (((1,), (1,)), ((), ())), preferred_element_type=F32)


def _dot_tn(a, b):
    return lax.dot_general(a, b, (((0,), (0,)), ((), ())), preferred_element_type=F32)


def _split3(x):
    hi = x.astype(BF16)
    r1 = x - hi.astype(F32)
    mid = r1.astype(BF16)
    lo = (r1 - mid.astype(F32)).astype(BF16)
    return hi, mid, lo


def _dot3_rhs(a_bf16, x):
    hi, mid, lo = _split3(x)
    return _dot(a_bf16, hi) + _dot(a_bf16, mid) + _dot(a_bf16, lo)


def _dot3_lhs(x, b_bf16):
    hi, mid, lo = _split3(x)
    return _dot(hi, b_bf16) + _dot(mid, b_bf16) + _dot(lo, b_bf16)


def _log_sigmoid(x):
    return jnp.minimum(x, 0.0) - jnp.log1p(jnp.exp(-jnp.abs(x)))


def _mod_rows(ref, per_row):
    if per_row:
        return ref[0]
    return ref[0, pl.ds(pl.program_id(0), 1), :]


def _modulated_norm(x, shift, scale):
    ms = jnp.mean(x * x, axis=-1, keepdims=True)
    return (x * lax.rsqrt(ms + EPS)) * (1.0 + scale) + shift


def _mod_specs(sub, per_row, n_rows, row_off):
    specs = []
    for k in range(3):
        j = sub * 3 + k
        if per_row:
            specs.append(pl.BlockSpec((1, n_rows, D_MODEL), lambda b, t, j=j: (j, 0, 0)))
        else:
            specs.append(pl.BlockSpec((1, SUBLANE, D_MODEL), lambda b, t, j=j: (j, row_off // SUBLANE, 0)))
    return specs


def _adaln_body(c_ref, w_ref, b_ref, o_ref):
    c = c_ref[...]
    s = c * jax.nn.sigmoid(c)
    o_ref[0] = _dot(s.astype(BF16), w_ref[...].astype(BF16)) + b_ref[...]


def _adaln(c_all, w_ada, b_ada):
    rows = c_all.shape[0]
    n = N_SUB * 3
    return pl.pallas_call(
        _adaln_body,
        out_shape=jax.ShapeDtypeStruct((n, rows, D_MODEL), F32),
        grid=(n,),
        in_specs=[pl.BlockSpec((rows, D_MODEL), lambda j: (0, 0)),
                  pl.BlockSpec((D_MODEL, D_MODEL), lambda j: (0, j)),
                  pl.BlockSpec((1, D_MODEL), lambda j: (0, j))],
        out_specs=pl.BlockSpec((1, rows, D_MODEL), lambda j: (j, 0, 0)),
        compiler_params=_cparams(("parallel",)),
        name="adaln",
    )(c_all, w_ada, b_ada.reshape(1, -1))


def _ffn_body(x_ref, sh_ref, sc_ref, gt_ref, wup_ref, wdn_ref, o_ref, *, per_row):
    x = x_ref[0]
    sh, sc, gt = (_mod_rows(r, per_row) for r in (sh_ref, sc_ref, gt_ref))
    h = _modulated_norm(x, sh, sc).astype(BF16)
    acc = None
    for c in range(D_FF // FF_CHUNK):
        lo = c * FF_CHUNK
        g = _dot(h, wup_ref[:, lo:lo + FF_CHUNK])
        u = _dot(h, wup_ref[:, D_FF + lo:D_FF + lo + FF_CHUNK])
        a = (g * jax.nn.sigmoid(g) * u).astype(BF16)
        part = _dot(a, wdn_ref[lo:lo + FF_CHUNK, :])
        acc = part if acc is None else acc + part
    o_ref[0] = x + (0.5 * gt) * acc


def _ffn(x, mod, sub, per_row, row_off, w_up, w_dn):
    nb, nt, _ = x.shape
    tm = min(FFN_TILE, nt)
    x_spec = pl.BlockSpec((1, tm, D_MODEL), lambda b, t: (b, t, 0))
    return pl.pallas_call(
        functools.partial(_ffn_body, per_row=per_row),
        out_shape=jax.ShapeDtypeStruct(x.shape, F32),
        grid=(nb, nt // tm),
        in_specs=[x_spec, *_mod_specs(sub, per_row, nt, row_off),
                  _const_spec((D_MODEL, 2 * D_FF)), _const_spec((D_FF, D_MODEL))],
        out_specs=x_spec,
        compiler_params=_cparams(("parallel", "parallel")),
        name="ffn",
    )(x, mod, mod, mod, w_up, w_dn)


def _inproj_body(*refs, per_row, fused):
    (x_ref, sh_ref, sc_ref, cos_ref, sin_ref, w_ref, bf_ref, qn_ref, kn_ref, bd_ref), refs = refs[:10], refs[10:]
    if fused:
        (rn_ref, ltri_ref, ones_ref, place_ref, row_ref, thr_ref), refs = refs[:6], refs[6:]
        (or_o, st_o, qa_o, ka_o, base_o, skip_o), refs = refs[:6], refs[6:]
    else:
        (rq_o, rk_o, rv_o, rg_o), refs = refs[:4], refs[4:]
    (fq_o, fkb_o, fvb_o, fkt_o, fvt_o, gr_o, gf_o, lft_o), refs = refs[:8], refs[8:]

    x = x_ref[0]
    sh, sc = (_mod_rows(r, per_row) for r in (sh_ref, sc_ref))
    h = _modulated_norm(x, sh, sc).astype(BF16)

    def seg(name):
        lo, hi = _SEG[name]
        return _dot(h, w_ref[:, lo:hi])

    cos, sin = cos_ref[...], sin_ref[...]

    def rope_heads(z, scale):
        heads = []
        for hd in range(H_R):
            zh = z[:, hd * DK_R:(hd + 1) * DK_R]
            r = zh * cos + pltpu.roll(zh, DK_R // 2, 1) * sin
            heads.append((r * scale if scale != 1.0 else r).astype(BF16))
        return heads

    q_heads = rope_heads(seg("rq"), 1.0)
    k_heads = rope_heads(seg("rk"), DK_R ** -0.5)
    rv = seg("rv").astype(BF16)
    rg = seg("rg")
    if fused:
        s_scr, carry, bases = refs
        _retention_block(q_heads, k_heads, rv, rg, rn_ref, or_o, st_o, s_scr)
    else:
        for hd in range(H_R):
            rq_o[0, :, hd * DK_R:(hd + 1) * DK_R] = q_heads[hd]
            rk_o[0, :, hd * DK_R:(hd + 1) * DK_R] = k_heads[hd]
        rv_o[0] = rv
        rg_o[0] = rg.astype(BF16)

    bd = bd_ref[...]

    def head_rms(z, w):
        ms = _dot((z * z).astype(BF16), bd)
        return (z * lax.rsqrt(ms + EPS)) * w

    fq = (head_rms(seg("fq"), qn_ref[...]) * (DH_F ** -0.5)).astype(BF16)
    fq_o[0] = fq if fused else fq.astype(F32).T
    fk = head_rms(seg("fk"), kn_ref[...])
    fkt_o[0] = fk.T
    fkb_o[0] = fk.astype(BF16)
    fv = seg("fv")
    fvt_o[0] = fv.T
    fvb_o[0] = fv.astype(BF16)
    gr_o[0] = jax.nn.sigmoid(seg("gr")).astype(BF16)
    gf_o[0] = jax.nn.sigmoid(seg("gf")).astype(BF16)

    logf = _log_sigmoid(seg("fl") + bf_ref[...])
    lane = lax.broadcasted_iota(jnp.int32, logf.shape, 1)
    logf = jnp.where(lane < H_F, logf, 0.0)
    lft_o[0] = logf.T[:H_F]
    if fused:
        _forget_block(logf, ltri_ref, ones_ref, place_ref, row_ref, thr_ref, qa_o, ka_o, base_o, skip_o, carry, bases)


def _inproj(x, mod, per_row, row_off, cos_t, sin_t, w_in, bf_row, qn, kn, bd, fused=None):
    nb, nt, _ = x.shape
    tm = min(ROW_TILE, nt)
    nblk = nt // tm
    w_aug = N_PAIR * LANE
    row = lambda w: pl.BlockSpec((1, tm, w), lambda b, t: (b, t, 0))
    col = lambda w: pl.BlockSpec((1, w, tm), lambda b, t: (b, 0, t))
    rows = lambda name, w: (name, jax.ShapeDtypeStruct((nb, nt, w), BF16), row(w))
    cols = lambda name, w: (name, jax.ShapeDtypeStruct((nb, w, nt), F32), col(w))
    stat = lambda name: (name, jax.ShapeDtypeStruct((nb, nblk, SUBLANE, LANE), F32),
                         pl.BlockSpec((1, 1, SUBLANE, LANE), lambda b, t: (b, t, 0, 0)))
    ins = [x, mod, mod, cos_t, sin_t, w_in, bf_row, qn, kn, bd]
    in_specs = [row(D_MODEL), *_mod_specs(1, per_row, nt, row_off)[:2],
                pl.BlockSpec((tm, DK_R), lambda b, t: (t, 0)), pl.BlockSpec((tm, DK_R), lambda b, t: (t, 0)),
                _const_spec((D_MODEL, D_IN_PAD)), _const_spec((1, LANE)), _const_spec((1, D_FQK)),
                _const_spec((1, D_FQK)), _const_spec((D_FQK, D_FQK))]
    scratch = []
    if fused is not None:
        assert tm == ATT_BLK and tm % RET_CHUNK == 0
        norm_w, thr = fused
        tables = _forget_tables()
        ins += [norm_w, *tables, jnp.broadcast_to(thr.astype(F32), (1, LANE))]
        in_specs += [_const_spec(a.shape) for a in ins[10:]]
        outs = [rows("o_r", D_RV),
                ("st", jax.ShapeDtypeStruct((nb, H_R, DK_R, DV_R), F32),
                 pl.BlockSpec((1, H_R, DK_R, DV_R), lambda b, t: (b, 0, 0, 0))),
                rows("qa", w_aug), rows("ka", w_aug), stat("base"), stat("skip")]
        scratch = [pltpu.VMEM((H_R, DK_R, DV_R), F32), pltpu.VMEM((1, LANE), F32),
                   pltpu.VMEM((-(-nblk // SUBLANE) * SUBLANE, LANE), F32)]
    else:
        outs = [rows("rq", D_RQK), rows("rk", D_RQK), rows("rv", D_RV), rows("rg", D_RV)]
    outs += [(rows if fused is not None else cols)("fq", D_FQK),
             rows("fkb", D_FQK), rows("fvb", D_FQK), cols("fkt", D_FQK), cols("fvt", D_FQK),
             rows("gr", D_MODEL), rows("gf", D_MODEL), cols("lft", H_F)]
    res = pl.pallas_call(
        functools.partial(_inproj_body, per_row=per_row, fused=fused is not None),
        out_shape=[o[1] for o in outs],
        grid=(nb, nblk),
        in_specs=in_specs,
        out_specs=[o[2] for o in outs],
        scratch_shapes=scratch,
        compiler_params=_cparams(("parallel", "arbitrary" if fused is not None else "parallel")),
        name="inproj",
    )(*ins)
    return {o[0]: r for o, r in zip(outs, res)}


def _forget_block(logf, ltri_ref, ones_ref, place_ref, row_ref, thr_ref, qa_o, ka_o, base_o, skip_o, carry, bases):
    i = pl.program_id(1)

    @pl.when(i == 0)
    def _():
        carry[...] = jnp.zeros_like(carry)
        bases[...] = jnp.zeros_like(bases)

    ltri = ltri_ref[...]
    w_aug = qa_o.shape[2]
    run = jnp.zeros((1, LANE), F32)
    for r in range(logf.shape[0] // LANE):
        rows = slice(r * LANE, (r + 1) * LANE)
        parts = _dot(ltri, jnp.concatenate(_split3(logf[rows]), axis=1))
        c = parts[:, :LANE] + parts[:, LANE:2 * LANE] + parts[:, 2 * LANE:] + run
        run = c[LANE - 1:LANE, :]
        placed = _dot(jnp.concatenate(_split3(c), axis=1), place_ref[...]) + row_ref[...]
        qa_o[0, rows, :] = placed[:, :w_aug].astype(BF16)
        ka_o[0, rows, :] = placed[:, w_aug:].astype(BF16)

    base = carry[...]
    rr = lax.broadcasted_iota(jnp.int32, (2 * SUBLANE, LANE), 0)
    cc = lax.broadcasted_iota(jnp.int32, (2 * SUBLANE, LANE), 1)
    diag = jnp.where(rr == cc, jnp.broadcast_to(base, (2 * SUBLANE, LANE)), 0.0)
    base_o[0, 0] = _dot3_lhs(diag, ones_ref[...])[:SUBLANE]
    carry[...] = base + run

    bases[pl.ds(i, 1), :] = base
    jj = lax.broadcasted_iota(jnp.int32, bases.shape, 0)
    far = (jj >= 1) & (jj <= i) & (bases[...] - base > thr_ref[...])
    skip_o[0, 0] = jnp.broadcast_to(jnp.sum(jnp.where(far, 1.0, 0.0), axis=0, keepdims=True), (SUBLANE, LANE))


def _forget_tables():
    w = N_PAIR * LANE
    pq, pk = np.zeros((3, LANE, w), np.float32), np.zeros((3, LANE, w), np.float32)
    qrow, krow = np.zeros((1, w), np.float32), np.zeros((1, w), np.float32)
    for h in range(H_F):
        l0 = (h // 2) * LANE + (h % 2) * DH_F
        for s in range(3):
            pq[s, h, l0 + s] = 1.0
            pk[s, h, l0 + 3 + s] = -1.0
            qrow[0, l0 + 3 + s] = 1.0
            krow[0, l0 + s] = 1.0
    r = np.arange(LANE)
    ltri = (r[None, :] <= r[:, None]).astype(np.float32)
    as_bf16 = lambda a: jnp.asarray(a, dtype=BF16)
    place = np.concatenate([pq.reshape(3 * LANE, w), pk.reshape(3 * LANE, w)], axis=1)
    return (as_bf16(ltri), jnp.ones((LANE, LANE), BF16), as_bf16(place),
            jnp.asarray(np.concatenate([qrow, krow], axis=1)))


def _fox_prompt_body(skip_ref, q_ref, qa_ref, k_ref, ka_ref, v_ref, base_ref, o_ref, m_scr, l_scr, acc_scr):
    b = pl.program_id(0)
    hp = pl.program_id(1)
    i = pl.program_id(2)
    tq = q_ref.shape[1]
    tk = tq
    lane2 = lax.broadcasted_iota(jnp.int32, (1, 2 * LANE), 1)
    first = (lane2 % LANE) < DH_F
    qf = jnp.concatenate([q_ref[0], qa_ref[0]], axis=1)
    zero = jnp.zeros_like(qf)
    q_heads = (jnp.where(first, qf, zero), jnp.where(first, zero, qf))

    m_scr[...] = jnp.full(m_scr.shape, NEG_INF, F32)
    l_scr[...] = jnp.zeros_like(l_scr)
    acc_scr[...] = jnp.zeros_like(acc_scr)

    def step(hh, j, masked):
        off = pl.multiple_of(j * tk, tk)
        kf = jnp.concatenate([k_ref[0, pl.ds(off, tk), :], ka_ref[0, pl.ds(off, tk), :]], axis=1)
        s = _dot_nt(q_heads[hh], kf)
        if masked:
            row = lax.broadcasted_iota(jnp.int32, (tq, tk), 0)
            col = lax.broadcasted_iota(jnp.int32, (tq, tk), 1)
            s = jnp.where(col <= row, s, NEG_INF)
        h = 2 * hp + hh
        d = base_ref[0, i, pl.ds(h, 1), :] - base_ref[0, j, pl.ds(h, 1), :]
        m_prev = m_scr[hh]
        m_next = jnp.maximum(m_prev, jnp.max(s, axis=1, keepdims=True) + d)
        alpha = jnp.exp(m_prev - m_next)
        sub = m_next - d
        p = jnp.exp(s - jnp.concatenate([sub] * (tk // LANE), axis=1))
        l_scr[hh] = alpha * l_scr[hh] + jnp.sum(p, axis=1, keepdims=True)
        acc_scr[hh] = alpha * acc_scr[hh] + _dot(p.astype(BF16), v_ref[0, pl.ds(off, tk), :])
        m_scr[hh] = m_next

    for hh in range(2):
        j0 = skip_ref[(b * pl.num_programs(2) + i) * H_F + 2 * hp + hh]

        def full_step(j, carry, hh=hh):
            step(hh, j, False)
            return carry

        lax.fori_loop(j0, i, full_step, 0)
        step(hh, i, True)

    lane = lax.broadcasted_iota(jnp.int32, (1, LANE), 1)
    o = jnp.where(lane < DH_F, acc_scr[0] / l_scr[0], acc_scr[1] / l_scr[1])
    o_ref[0] = o.astype(o_ref.dtype)


def _fox_prompt(skip, fq, qa, fkb, ka, fvb, base, blk):
    nb, nt, _ = fq.shape
    nblk = nt // blk
    qspec = pl.BlockSpec((1, blk, LANE), lambda b, p, i, sk: (b, i, p))
    kspec = pl.BlockSpec((1, nt, LANE), lambda b, p, i, sk: (b, 0, p))
    grid_spec = pltpu.PrefetchScalarGridSpec(
        num_scalar_prefetch=1,
        grid=(nb, N_PAIR, nblk),
        in_specs=[qspec, qspec, kspec, kspec, kspec,
                  pl.BlockSpec((1, nblk, SUBLANE, LANE), lambda b, p, i, sk: (b, 0, 0, 0))],
        out_specs=qspec,
        scratch_shapes=[pltpu.VMEM((2, blk, LANE), F32)] * 3,
    )
    return pl.pallas_call(
        _fox_prompt_body,
        out_shape=jax.ShapeDtypeStruct((nb, nt, D_FQK), BF16),
        grid_spec=grid_spec,
        compiler_params=_cparams(("parallel", "parallel", "parallel")),
        name="fox_prompt",
    )(skip, fq, qa, fkb, ka, fvb, base)


def _log_gamma(h):
    return math.log(1.0 - 2.0 ** (-5.0 - h))


def _group_norm_gate(o, w, rg):
    mu = jnp.mean(o, axis=-1, keepdims=True)
    var = jnp.mean(jnp.square(o - mu), axis=-1, keepdims=True)
    on = (o - mu) * lax.rsqrt(var + EPS)
    rg = rg.astype(F32)
    return (rg * jax.nn.sigmoid(rg)) * (on * w)


def _retention_block(q_heads, k_heads, v_all, g_all, w_ref, o_ref, st_ref, s_scr):
    t = pl.program_id(1)

    @pl.when(t == 0)
    def _():
        s_scr[...] = jnp.zeros_like(s_scr)

    L = RET_CHUNK
    ri = lax.broadcasted_iota(jnp.int32, (L, L), 0).astype(F32)
    ci = lax.broadcasted_iota(jnp.int32, (L, L), 1).astype(F32)
    diff = ri - ci
    for c in range(v_all.shape[0] // L):
        rows = slice(c * L, (c + 1) * L)
        for h in range(H_R):
            lg = _log_gamma(h)
            vc = slice(h * DV_R, (h + 1) * DV_R)
            q = q_heads[h][rows]
            k = k_heads[h][rows]
            v = v_all[rows, vc]
            decay = jnp.where(diff >= 0, jnp.exp(lg * jnp.maximum(diff, 0.0)), 0.0)
            scores = _dot_nt(q, k) * decay
            s0 = s_scr[h]
            qd = (q.astype(F32) * jnp.exp(lg * (ri + 1.0))).astype(BF16)
            o = _dot(jnp.concatenate([scores.astype(BF16), qd], axis=1),
                     jnp.concatenate([v, s0.astype(BF16)], axis=0))
            kd = (k.astype(F32) * jnp.exp(lg * (L - 1.0 - ri))).astype(BF16)
            s_scr[h] = s0 * math.exp(lg * L) + _dot_tn(kd, v)
            o_ref[0, rows, vc] = _group_norm_gate(o, w_ref[:, vc], g_all[rows, vc]).astype(o_ref.dtype)

    @pl.when(t == pl.num_programs(1) - 1)
    def _():
        st_ref[0] = s_scr[...]


def _ret_decode_body(q_ref, k_ref, v_ref, g_ref, w_ref, s_ref, o_ref, so_ref):
    rr = lax.broadcasted_iota(jnp.int32, (DK_R, DK_R), 0)
    cc = lax.broadcasted_iota(jnp.int32, (DK_R, DK_R), 1)
    eye = rr == cc
    for n in range(q_ref.shape[0]):
        for h in range(H_R):
            gamma = math.exp(_log_gamma(h))
            kc = slice(h * DK_R, (h + 1) * DK_R)
            vc = slice(h * DV_R, (h + 1) * DV_R)
            q = q_ref[n, :, kc]
            k = k_ref[n, :, kc]
            v = v_ref[n, :, vc]
            s0 = s_ref[n, h]
            qk = jnp.sum(q.astype(F32) * k.astype(F32), axis=-1, keepdims=True)
            q_s = _dot(jnp.broadcast_to(q, (2 * SUBLANE, DK_R)), s0.astype(BF16))[:1]
            o = qk.astype(BF16).astype(F32) * v.astype(F32) + q_s * gamma
            k_diag = jnp.where(eye, jnp.broadcast_to(k.astype(F32), (DK_R, DK_R)), 0.0).astype(BF16)
            v_rows = jnp.broadcast_to(v.astype(F32), (DK_R, DV_R)).astype(BF16)
            so_ref[n, h] = s0 * gamma + _dot(k_diag, v_rows)
            o_ref[n, :, vc] = _group_norm_gate(o, w_ref[:, vc], g_ref[n, :, vc]).astype(o_ref.dtype)


def _ret_decode(rq, rk, rv, rg, norm_w, state):
    ns = rq.shape[0]
    sb = min(DEC_SEQS, ns)
    spec = lambda w: pl.BlockSpec((sb, 1, w), lambda i: (i, 0, 0))
    st_spec = pl.BlockSpec((sb, H_R, DK_R, DV_R), lambda i: (i, 0, 0, 0))
    return pl.pallas_call(
        _ret_decode_body,
        out_shape=[jax.ShapeDtypeStruct((ns, 1, D_RV), BF16), jax.ShapeDtypeStruct(state.shape, F32)],
        grid=(ns // sb,),
        in_specs=[spec(D_RQK), spec(D_RQK), spec(D_RV), spec(D_RV), _const_spec((1, D_RV)), st_spec],
        out_specs=[spec(D_RV), st_spec],
        compiler_params=_cparams(("parallel",)),
        name="ret_decode",
    )(rq, rk, rv, rg, norm_w, state)


def _page_totals_body(lf_ref, ones_ref, o_ref):
    ones = ones_ref[...]
    for c in range(o_ref.shape[0]):
        hi, mid, lo = _split3(lf_ref[c * LANE:(c + 1) * LANE, :])
        o_ref[c:c + 1, :] = (_dot_nt(ones, hi) + _dot_nt(ones, mid) + _dot_nt(ones, lo))[:1]


def _page_totals(clf_t):
    n_phys = clf_t.shape[0]
    rows = clf_t.reshape(n_phys * H_F, PAGE_SIZE)
    blk = SUBLANE * LANE
    n_rows = -(-rows.shape[0] // blk) * blk
    if n_rows != rows.shape[0]:
        rows = jnp.pad(rows, ((0, n_rows - rows.shape[0]), (0, 0)))
    step = max(d for d in range(blk, 8 * blk + 1, blk) if n_rows % d == 0)
    tot = pl.pallas_call(
        _page_totals_body,
        out_shape=jax.ShapeDtypeStruct((n_rows // LANE, LANE), F32),
        grid=(n_rows // step,),
        in_specs=[pl.BlockSpec((step, PAGE_SIZE), lambda i: (i, 0)), _const_spec((2 * SUBLANE, LANE))],
        out_specs=pl.BlockSpec((step // LANE, LANE), lambda i: (i, 0)),
        compiler_params=_cparams(("parallel",)),
        name="page_totals",
    )(rows, jnp.ones((2 * SUBLANE, LANE), BF16))
    return tot.reshape(-1)[:n_phys * H_F].reshape(n_phys, H_F)


def _fox_decode_body(pt_ref, ng_ref, q_ref, kn_ref, vn_ref, lfn_ref, msuf_ref, ones_ref, pex_ref,
                     ck_hbm, cv_hbm, clf_hbm, o_ref,
                     kbuf, vbuf, lfbuf, sem, m_run, l_run, tail, acc_run, qrep, count, *, n_pages, n_tab):
    s = pl.program_id(0)
    ng = ng_ref[s]
    stat = (SUBLANE, LANE)

    def page_copies(seq, group, slot):
        copies = []
        for r in range(n_pages):
            page = pt_ref[seq, n_tab - 1 - (group * n_pages + r)]
            for kind, (hbm, buf) in enumerate(((ck_hbm, kbuf), (cv_hbm, vbuf), (clf_hbm, lfbuf))):
                copies.append(pltpu.make_async_copy(hbm.at[page], buf.at[slot, r], sem.at[slot, kind]))
        return copies

    def head_sums(x):
        return x.reshape(H_F, DH_F // SUBLANE, SUBLANE, LANE).sum(axis=1).sum(axis=1)

    @pl.when(s == 0)
    def _():
        count[0] = 0
        for c in page_copies(0, 0, 0):
            c.start()

    def column(ref):
        x = ref[0]
        mine = lax.broadcasted_iota(jnp.int32, x.shape, 1) == s
        return jnp.sum(jnp.where(mine, x, 0.0), axis=1, keepdims=True)

    first = count[0]
    q = jnp.broadcast_to(column(q_ref), (D_FQK, LANE))
    qrep[...] = q
    s_self = head_sums(q * column(kn_ref))
    v_new = column(vn_ref)
    m_run[...] = s_self
    l_run[...] = jnp.zeros(stat, F32)
    acc_run[...] = jnp.zeros_like(acc_run)
    tail[...] = jnp.broadcast_to(column(lfn_ref), stat)

    def group_step(g, carry):
        slot = lax.rem(first + g, 2)

        @pl.when(g + 1 < ng)
        def _():
            for c in page_copies(s, g + 1, 1 - slot):
                c.start()

        @pl.when(jnp.logical_and(g + 1 == ng, s + 1 < pl.num_programs(0)))
        def _():
            for c in page_copies(s + 1, 0, 1 - slot):
                c.start()

        for c in page_copies(s, g, slot):
            c.wait()

        qv = qrep[...]
        rows = [lfbuf[slot, r] for r in range(n_pages)]
        rows.append(jnp.zeros((LANE - n_pages * SUBLANE, LANE), F32))
        lf = jnp.concatenate(rows, axis=0)
        suf = _dot3_lhs(lf, msuf_ref[...])
        tot = _dot3_lhs(lf, ones_ref[...])
        pre = _dot3_rhs(pex_ref[...], tot)
        t_in = tail[...]
        last = slice((n_pages - 1) * SUBLANE, n_pages * SUBLANE)
        tail[...] = t_in + pre[last] + tot[last]
        s_pages = []
        m_step = jnp.full((SUBLANE, 1), NEG_INF, F32)
        for r in range(n_pages):
            pr = slice(r * SUBLANE, (r + 1) * SUBLANE)
            sr = head_sums(kbuf[slot, r] * qv) + (suf[pr] + pre[pr] + t_in)
            s_pages.append(sr)
            m_step = jnp.maximum(m_step, jnp.max(sr, axis=1, keepdims=True))
        m_prev = m_run[...]
        m_new = jnp.maximum(m_prev, m_step)
        alpha = jnp.exp(m_prev - m_new)
        p_pages = [jnp.exp(sr - m_new) for sr in s_pages]
        l_step = jnp.zeros((SUBLANE, 1), F32)
        for p in p_pages:
            l_step = l_step + jnp.sum(p, axis=1, keepdims=True)
        for h in range(H_F):
            hr = slice(h * DH_F, (h + 1) * DH_F)
            acc = acc_run[hr, :] * alpha[h:h + 1, :]
            for r in range(n_pages):
                acc = acc + vbuf[slot, r, hr, :] * p_pages[r][h:h + 1, :]
            acc_run[hr, :] = acc
        l_run[...] = alpha * l_run[...] + l_step
        m_run[...] = m_new
        return carry

    lax.fori_loop(0, ng, group_step, 0)
    count[0] = first + ng

    w_self = jnp.exp(s_self - m_run[...])
    l_tot = l_run[...] + w_self
    for h in range(H_F):
        hr = slice(h * DH_F, (h + 1) * DH_F)
        acc_col = jnp.sum(acc_run[hr, :], axis=1, keepdims=True)
        o_ref[0, hr, :] = (acc_col + w_self[h:h + 1, :1] * v_new[hr, :]) / l_tot[h:h + 1, :1]


def _fox_decode(page_table, fq_t, fk_t, fv_t, lfn_t, ck_t, cv_t, clf_t, thr):
    ns, n_tab = page_table.shape
    pg = min(DEC_PAGES, n_tab)
    assert pg * SUBLANE <= LANE and n_tab % pg == 0

    tot = _page_totals(clf_t)[page_table]
    after = jnp.transpose(lfn_t[0])[:, None, :] + jnp.cumsum(tot[:, ::-1], axis=1)[:, ::-1] - tot
    need = jnp.maximum(jnp.sum(jnp.any(after >= -thr, axis=-1), axis=1), 1)
    n_groups = ((need + pg - 1) // pg).astype(jnp.int32)

    r = np.arange(LANE)
    msuf = jnp.asarray((r[:, None] > r[None, :]).astype(np.float32), dtype=BF16)
    same_head = (r[:, None] % SUBLANE) == (r[None, :] % SUBLANE)
    in_step = (r[:, None] < pg * SUBLANE) & (r[None, :] < pg * SUBLANE)
    pex = jnp.asarray((same_head & in_step & (r[None, :] // SUBLANE < r[:, None] // SUBLANE)).astype(np.float32),
                      dtype=BF16)
    hbm = pl.BlockSpec(memory_space=pl.ANY)
    stat = pltpu.VMEM((SUBLANE, LANE), F32)
    grid_spec = pltpu.PrefetchScalarGridSpec(
        num_scalar_prefetch=2,
        grid=(ns,),
        in_specs=[_const_spec((1, D_FQK, ns)), _const_spec((1, D_FQK, ns)), _const_spec((1, D_FQK, ns)),
                  _const_spec((1, H_F, ns)), _const_spec((LANE, LANE)),
                  _const_spec((LANE, LANE)), _const_spec((LANE, LANE)), hbm, hbm, hbm],
        out_specs=pl.BlockSpec((1, D_FQK, 1), lambda s, pt, ng: (s, 0, 0)),
        scratch_shapes=[pltpu.VMEM((2, pg, D_FQK, PAGE_SIZE), F32), pltpu.VMEM((2, pg, D_FQK, PAGE_SIZE), F32),
                        pltpu.VMEM((2, pg, H_F, PAGE_SIZE), F32), pltpu.SemaphoreType.DMA((2, 3)),
                        stat, stat, stat, pltpu.VMEM((D_FQK, LANE), F32), pltpu.VMEM((D_FQK, LANE), F32),
                        pltpu.SMEM((1,), jnp.int32)],
    )
    return pl.pallas_call(
        functools.partial(_fox_decode_body, n_pages=pg, n_tab=n_tab),
        out_shape=jax.ShapeDtypeStruct((ns, D_FQK, 1), F32),
        grid_spec=grid_spec,
        compiler_params=_cparams(("arbitrary",)),
        name="fox_decode",
    )(page_table, n_groups, fq_t, fk_t, fv_t, lfn_t, msuf, jnp.ones((LANE, LANE), BF16), pex,
      ck_t, cv_t, clf_t)


def _outproj_body(x_ref, gt_ref, or_ref, of_ref, gr_ref, gf_ref, wb_ref, wo_ref, o_ref, *, per_row):
    gt = _mod_rows(gt_ref, per_row)
    merged = (gr_ref[0].astype(F32) * _dot(or_ref[0], wb_ref[:D_RV, :])
              + gf_ref[0].astype(F32) * _dot(of_ref[0], wb_ref[D_RV:, :]))
    y = _dot(merged.astype(BF16), wo_ref[...])
    o_ref[0] = x_ref[0] + gt * y


def _outproj(x, mod, per_row, row_off, o_r, o_f, gr, gf, w_branch, w_out):
    nb, nt, _ = x.shape
    tm = min(FFN_TILE, nt)
    row = lambda w: pl.BlockSpec((1, tm, w), lambda b, t: (b, t, 0))
    return pl.pallas_call(
        functools.partial(_outproj_body, per_row=per_row),
        out_shape=jax.ShapeDtypeStruct(x.shape, F32),
        grid=(nb, nt // tm),
        in_specs=[row(D_MODEL), _mod_specs(1, per_row, nt, row_off)[2], row(D_RV), row(D_FQK),
                  row(D_MODEL), row(D_MODEL), _const_spec((D_RV + D_FQK, D_MODEL)),
                  _const_spec((D_MODEL, D_MODEL))],
        out_specs=row(D_MODEL),
        compiler_params=_cparams(("parallel", "parallel")),
        name="outproj",
    )(x, mod, o_r, o_f, gr, gf, w_branch, w_out)


def _rope_tables(pos):
    half = DK_R // 2
    inv = ROPE_BASE ** (-jnp.arange(half, dtype=F32) / half)
    ang = pos.astype(F32)[:, None] * inv[None, :]
    cos, sin = jnp.cos(ang), jnp.sin(ang)
    return jnp.concatenate([cos, cos], axis=1), jnp.concatenate([-sin, sin], axis=1)


def _permute_w_in(w_in):
    sizes = (D_RQK, D_RQK, D_RV, D_RV, D_FQK, D_FQK, D_FQK, H_F, D_MODEL, D_MODEL)
    offs = [0]
    for s in sizes:
        offs.append(offs[-1] + s)
    part = lambda i: w_in[:, offs[i]:offs[i + 1]]
    fl = jnp.pad(part(7), ((0, 0), (0, LANE - H_F)))
    return jnp.concatenate([part(i) for i in (0, 1, 2, 3, 4, 5, 6, 8, 9)] + [fl], axis=1).astype(BF16)


def kernel(x_prompt, x_sample, cache_k, cache_v, cache_logf, state_ret, page_table, c_prompt, c_sample,
           w_ada, b_ada, w_ffa_up, w_ffa_down, w_in, b_forget, q_norm_w, k_norm_w, ret_norm_w,
           w_branch, w_out, w_ffb_up, w_ffb_down):
    nb, nt, _ = x_prompt.shape
    ns = x_sample.shape[0]
    depth = w_ada.shape[0]
    past_len = page_table.shape[1] * PAGE_SIZE
    assert x_sample.shape[1] == 1 and nb <= SUBLANE and ns % SUBLANE == 0 and nt % ATT_BLK == 0

    pad = (-(ns + SUBLANE)) % (2 * SUBLANE)
    c_all = jnp.concatenate([c_sample, c_prompt, jnp.zeros((SUBLANE - nb + pad, D_MODEL), F32)], axis=0)
    cos_p, sin_p = _rope_tables(jnp.arange(nt, dtype=jnp.int32))
    cos_s, sin_s = _rope_tables(jnp.full((ns,), past_len, jnp.int32))
    hd = np.arange(D_FQK) // DH_F
    bd = jnp.asarray((hd[:, None] == hd[None, :]).astype(np.float32) / DH_F, dtype=BF16)

    y_p = x_prompt
    y_s = x_sample.reshape(1, ns, D_MODEL)
    outs = [[] for _ in range(8)]
    for l in range(depth):
        bf = lambda w: w[l].astype(BF16)
        w_up_a, w_dn_a, w_up_b, w_dn_b = bf(w_ffa_up), bf(w_ffa_down), bf(w_ffb_up), bf(w_ffb_down)
        w_br, w_o = bf(w_branch), bf(w_out)
        w_inp = _permute_w_in(w_in[l])
        bf_row = jnp.pad(b_forget[l], (0, LANE - H_F)).reshape(1, LANE)
        qn = jnp.tile(q_norm_w[l], H_F).reshape(1, D_FQK)
        kn = jnp.tile(k_norm_w[l], H_F).reshape(1, D_FQK)
        rn = ret_norm_w[l].reshape(1, D_RV)
        mod = _adaln(c_all, w_ada[l], b_ada[l])
        qk_bound = 1.02 * DH_F ** 0.5 * jnp.max(jnp.abs(q_norm_w[l])) * jnp.max(jnp.abs(k_norm_w[l]))
        thr = EXP_UNDERFLOW + 2.0 * qk_bound
        page_view = lambda c, w: jnp.transpose(c, (0, 2, 3, 1)).reshape(c.shape[0], w, PAGE_SIZE)

        y_p = _ffn(y_p, mod, 0, False, ns, w_up_a, w_dn_a)
        pj = _inproj(y_p, mod, False, ns, cos_p, sin_p, w_inp, bf_row, qn, kn, bd, fused=(rn, thr))
        skip = pj["skip"][:, :, 0, :H_F].astype(jnp.int32).reshape(-1)
        o_f = _fox_prompt(skip, pj["fq"], pj["qa"], pj["fkb"], pj["ka"], pj["fvb"], pj["base"], ATT_BLK)
        st_p = pj["st"]
        y_p = _outproj(y_p, mod, False, ns, pj["o_r"], o_f, pj["gr"], pj["gf"], w_br, w_o)
        y_p = _ffn(y_p, mod, 2, False, ns, w_up_b, w_dn_b)

        y_s = _ffn(y_s, mod, 0, True, 0, w_up_a, w_dn_a)
        sj = _inproj(y_s, mod, True, 0, cos_s, sin_s, w_inp, bf_row, qn, kn, bd)
        seq = lambda a: a.reshape(ns, 1, a.shape[-1])
        o_f_s = _fox_decode(page_table, sj["fq"], sj["fkt"], sj["fvt"], sj["lft"],
                            page_view(cache_k[l], D_FQK), page_view(cache_v[l], D_FQK),
                            jnp.transpose(cache_logf[l], (0, 2, 1)), thr)
        o_r_s, st_s = _ret_decode(seq(sj["rq"]), seq(sj["rk"]), seq(sj["rv"]), seq(sj["rg"]), rn, state_ret[l])
        y_s = _outproj(y_s, mod, True, 0, o_r_s.reshape(1, ns, D_RV), o_f_s.reshape(1, ns, D_FQK).astype(BF16),
                       sj["gr"], sj["gf"], w_br, w_o)
        y_s = _ffn(y_s, mod, 2, True, 0, w_up_b, w_dn_b)

        rows = lambda a: jnp.transpose(a.reshape(a.shape[0], H_F, DH_F, a.shape[2]), (0, 3, 1, 2))
        for dst, val in zip(outs, (rows(pj["fkt"]), rows(pj["fvt"]), jnp.transpose(pj["lft"], (0, 2, 1)), st_p,
                                   rows(sj["fkt"]).reshape(ns, 1, H_F, DH_F),
                                   rows(sj["fvt"]).reshape(ns, 1, H_F, DH_F),
                                   jnp.transpose(sj["lft"], (0, 2, 1)).reshape(ns, 1, H_F), st_s)):
            dst.append(val)

    return (y_p, y_s.reshape(ns, 1, D_MODEL), *[jnp.stack(o) for o in outs])
```

```python
import functools
import math

import jax
import jax.numpy as jnp
import numpy as np
from jax import lax
from jax.experimental import pallas as pl
from jax.experimental.pallas import tpu as pltpu

D_MODEL = 1024
PAGE_SIZE = 128
H_R, DK_R, DV_R = 4, 128, 256
RET_CHUNK = 128
ROPE_BASE = 10000.0
H_F, DH_F = 8, 64
D_FF = 2816
EPS = 1e-6
N_SUB = 3
NEG_INF = -1e30
EXP_UNDERFLOW = 106.0

D_RQK = H_R * DK_R
D_RV = H_R * DV_R
D_FQK = H_F * DH_F
N_PAIR = H_F // 2
LANE = 128
SUBLANE = 8

F32 = jnp.float32
BF16 = jnp.bfloat16

_SEG = {}
_off = 0
for _name, _w in (("rq", D_RQK), ("rk", D_RQK), ("rv", D_RV), ("rg", D_RV), ("fq", D_FQK), ("fk", D_FQK),
                  ("fv", D_FQK), ("gr", D_MODEL), ("gf", D_MODEL), ("fl", LANE)):
    _SEG[_name] = (_off, _off + _w)
    _off += _w
D_IN_PAD = _off

ROW_TILE = 512
ATT_BLK = 512
FFN_TILE = 1024
FF_CHUNK = 256
RET_BLOCK = 512
DEC_PAGES = 4
DEC_SEQS = 4
VMEM_LIMIT = 56 * 2 ** 20


def _cparams(sem):
    return pltpu.CompilerParams(dimension_semantics=sem, vmem_limit_bytes=VMEM_LIMIT)


def _const_spec(shape):
    nd = len(shape)
    return pl.BlockSpec(shape, lambda *_: (0,) * nd, pipeline_mode=pl.Buffered(1))


def _dot(a, b):
    return jnp.dot(a, b, preferred_element_type=F32)


def _dot_nt(a, b):
    return lax.dot_general(a, b, (((1,), (1,)), ((), ())), preferred_element_type=F32)


def _dot_tn(a, b):
    return lax.dot_general(a, b, (((0,), (0,)), ((), ())), preferred_element_type=F32)


def _split3(x):
    hi = x.astype(BF16)
    r1 = x - hi.astype(F32)
    mid = r1.astype(BF16)
    lo = (r1 - mid.astype(F32)).astype(BF16)
    return hi, mid, lo


def _dot3_rhs(a_bf16, x):
    hi, mid, lo = _split3(x)
    return _dot(a_bf16, hi) + _dot(a_bf16, mid) + _dot(a_bf16, lo)


def _dot3_lhs(x, b_bf16):
    hi, mid, lo = _split3(x)
    return _dot(hi, b_bf16) + _dot(mid, b_bf16) + _dot(lo, b_bf16)


def _log_sigmoid(x):
    return jnp.minimum(x, 0.0) - jnp.log1p(jnp.exp(-jnp.abs(x)))


def _mod_rows(ref, per_row):
    if per_row:
        return ref[0]
    return ref[0, pl.ds(pl.program_id(0), 1), :]


def _modulated_norm(x, shift, scale):
    ms = jnp.mean(x * x, axis=-1, keepdims=True)
    return (x * lax.rsqrt(ms + EPS)) * (1.0 + scale) + shift


def _mod_specs(sub, per_row, n_rows, row_off):
    specs = []
    for k in range(3):
        j = sub * 3 + k
        if per_row:
            specs.append(pl.BlockSpec((1, n_rows, D_MODEL), lambda b, t, j=j: (j, 0, 0)))
        else:
            specs.append(pl.BlockSpec((1, SUBLANE, D_MODEL), lambda b, t, j=j: (j, row_off // SUBLANE, 0)))
    return specs


def _adaln_body(c_ref, w_ref, b_ref, o_ref):
    c = c_ref[...]
    s = c * jax.nn.sigmoid(c)
    o_ref[0] = _dot(s.astype(BF16), w_ref[...].astype(BF16)) + b_ref[...]


def _adaln(c_all, w_ada, b_ada):
    rows = c_all.shape[0]
    n = N_SUB * 3
    return pl.pallas_call(
        _adaln_body,
        out_shape=jax.ShapeDtypeStruct((n, rows, D_MODEL), F32),
        grid=(n,),
        in_specs=[pl.BlockSpec((rows, D_MODEL), lambda j: (0, 0)),
                  pl.BlockSpec((D_MODEL, D_MODEL), lambda j: (0, j)),
                  pl.BlockSpec((1, D_MODEL), lambda j: (0, j))],
        out_specs=pl.BlockSpec((1, rows, D_MODEL), lambda j: (j, 0, 0)),
        compiler_params=_cparams(("parallel",)),
        name="adaln",
    )(c_all, w_ada, b_ada.reshape(1, -1))


def _ffn_body(x_ref, sh_ref, sc_ref, gt_ref, wup_ref, wdn_ref, o_ref, *, per_row):
    x = x_ref[0]
    sh, sc, gt = (_mod_rows(r, per_row) for r in (sh_ref, sc_ref, gt_ref))
    h = _modulated_norm(x, sh, sc).astype(BF16)
    acc = None
    for c in range(D_FF // FF_CHUNK):
        lo = c * FF_CHUNK
        g = _dot(h, wup_ref[:, lo:lo + FF_CHUNK])
        u = _dot(h, wup_ref[:, D_FF + lo:D_FF + lo + FF_CHUNK])
        a = (g * jax.nn.sigmoid(g) * u).astype(BF16)
        part = _dot(a, wdn_ref[lo:lo + FF_CHUNK, :])
        acc = part if acc is None else acc + part
    o_ref[0] = x + (0.5 * gt) * acc


def _ffn(x, mod, sub, per_row, row_off, w_up, w_dn):
    nb, nt, _ = x.shape
    tm = min(FFN_TILE, nt)
    x_spec = pl.BlockSpec((1, tm, D_MODEL), lambda b, t: (b, t, 0))
    return pl.pallas_call(
        functools.partial(_ffn_body, per_row=per_row),
        out_shape=jax.ShapeDtypeStruct(x.shape, F32),
        grid=(nb, nt // tm),
        in_specs=[x_spec, *_mod_specs(sub, per_row, nt, row_off),
                  _const_spec((D_MODEL, 2 * D_FF)), _const_spec((D_FF, D_MODEL))],
        out_specs=x_spec,
        compiler_params=_cparams(("parallel", "parallel")),
        name="ffn",
    )(x, mod, mod, mod, w_up, w_dn)


def _inproj_body(*refs, per_row, fused):
    (x_ref, sh_ref, sc_ref, cos_ref, sin_ref, w_ref, bf_ref, qn_ref, kn_ref, bd_ref), refs = refs[:10], refs[10:]
    if fused:
        (rn_ref, ltri_ref, ones_ref, place_ref, row_ref, thr_ref), refs = refs[:6], refs[6:]
        (or_o, st_o, qa_o, ka_o, base_o, skip_o), refs = refs[:6], refs[6:]
    else:
        (rq_o, rk_o, rv_o, rg_o), refs = refs[:4], refs[4:]
    (fq_o, fkb_o, fvb_o, fkt_o, fvt_o, gr_o, gf_o, lft_o), refs = refs[:8], refs[8:]

    x = x_ref[0]
    sh, sc = (_mod_rows(r, per_row) for r in (sh_ref, sc_ref))
    h = _modulated_norm(x, sh, sc).astype(BF16)

    def seg(name):
        lo, hi = _SEG[name]
        return _dot(h, w_ref[:, lo:hi])

    cos, sin = cos_ref[...], sin_ref[...]

    def rope_heads(z, scale):
        heads = []
        for hd in range(H_R):
            zh = z[:, hd * DK_R:(hd + 1) * DK_R]
            r = zh * cos + pltpu.roll(zh, DK_R // 2, 1) * sin
            heads.append((r * scale if scale != 1.0 else r).astype(BF16))
        return heads

    q_heads = rope_heads(seg("rq"), 1.0)
    k_heads = rope_heads(seg("rk"), DK_R ** -0.5)
    rv = seg("rv").astype(BF16)
    rg = seg("rg")
    if fused:
        s_scr, carry, bases = refs
        _retention_block(q_heads, k_heads, rv, rg, rn_ref, or_o, st_o, s_scr)
    else:
        for hd in range(H_R):
            rq_o[0, :, hd * DK_R:(hd + 1) * DK_R] = q_heads[hd]
            rk_o[0, :, hd * DK_R:(hd + 1) * DK_R] = k_heads[hd]
        rv_o[0] = rv
        rg_o[0] = rg.astype(BF16)

    bd = bd_ref[...]

    def head_rms(z, w):
        ms = _dot((z * z).astype(BF16), bd)
        return (z * lax.rsqrt(ms + EPS)) * w

    fq = (head_rms(seg("fq"), qn_ref[...]) * (DH_F ** -0.5)).astype(BF16)
    fq_o[0] = fq if fused else fq.astype(F32).T
    fk = head_rms(seg("fk"), kn_ref[...])
    fkt_o[0] = fk.T
    fkb_o[0] = fk.astype(BF16)
    fv = seg("fv")
    fvt_o[0] = fv.T
    fvb_o[0] = fv.astype(BF16)
    gr_o[0] = jax.nn.sigmoid(seg("gr")).astype(BF16)
    gf_o[0] = jax.nn.sigmoid(seg("gf")).astype(BF16)

    logf = _log_sigmoid(seg("fl") + bf_ref[...])
    lane = lax.broadcasted_iota(jnp.int32, logf.shape, 1)
    logf = jnp.where(lane < H_F, logf, 0.0)
    lft_o[0] = logf.T[:H_F]
    if fused:
        _forget_block(logf, ltri_ref, ones_ref, place_ref, row_ref, thr_ref, qa_o, ka_o, base_o, skip_o, carry, bases)


def _inproj(x, mod, per_row, row_off, cos_t, sin_t, w_in, bf_row, qn, kn, bd, fused=None):
    nb, nt, _ = x.shape
    tm = min(ROW_TILE, nt)
    nblk = nt // tm
    w_aug = N_PAIR * LANE
    row = lambda w: pl.BlockSpec((1, tm, w), lambda b, t: (b, t, 0))
    col = lambda w: pl.BlockSpec((1, w, tm), lambda b, t: (b, 0, t))
    rows = lambda name, w: (name, jax.ShapeDtypeStruct((nb, nt, w), BF16), row(w))
    cols = lambda name, w: (name, jax.ShapeDtypeStruct((nb, w, nt), F32), col(w))
    stat = lambda name: (name, jax.ShapeDtypeStruct((nb, nblk, SUBLANE, LANE), F32),
                         pl.BlockSpec((1, 1, SUBLANE, LANE), lambda b, t: (b, t, 0, 0)))
    ins = [x, mod, mod, cos_t, sin_t, w_in, bf_row, qn, kn, bd]
    in_specs = [row(D_MODEL), *_mod_specs(1, per_row, nt, row_off)[:2],
                pl.BlockSpec((tm, DK_R), lambda b, t: (t, 0)), pl.BlockSpec((tm, DK_R), lambda b, t: (t, 0)),
                _const_spec((D_MODEL, D_IN_PAD)), _const_spec((1, LANE)), _const_spec((1, D_FQK)),
                _const_spec((1, D_FQK)), _const_spec((D_FQK, D_FQK))]
    scratch = []
    if fused is not None:
        assert tm == ATT_BLK and tm % RET_CHUNK == 0
        norm_w, thr = fused
        tables = _forget_tables()
        ins += [norm_w, *tables, jnp.broadcast_to(thr.astype(F32), (1, LANE))]
        in_specs += [_const_spec(a.shape) for a in ins[10:]]
        outs = [rows("o_r", D_RV),
                ("st", jax.ShapeDtypeStruct((nb, H_R, DK_R, DV_R), F32),
                 pl.BlockSpec((1, H_R, DK_R, DV_R), lambda b, t: (b, 0, 0, 0))),
                rows("qa", w_aug), rows("ka", w_aug), stat("base"), stat("skip")]
        scratch = [pltpu.VMEM((H_R, DK_R, DV_R), F32), pltpu.VMEM((1, LANE), F32),
                   pltpu.VMEM((-(-nblk // SUBLANE) * SUBLANE, LANE), F32)]
    else:
        outs = [rows("rq", D_RQK), rows("rk", D_RQK), rows("rv", D_RV), rows("rg", D_RV)]
    outs += [(rows if fused is not None else cols)("fq", D_FQK),
             rows("fkb", D_FQK), rows("fvb", D_FQK), cols("fkt", D_FQK), cols("fvt", D_FQK),
             rows("gr", D_MODEL), rows("gf", D_MODEL), cols("lft", H_F)]
    res = pl.pallas_call(
        functools.partial(_inproj_body, per_row=per_row, fused=fused is not None),
        out_shape=[o[1] for o in outs],
        grid=(nb, nblk),
        in_specs=in_specs,
        out_specs=[o[2] for o in outs],
        scratch_shapes=scratch,
        compiler_params=_cparams(("parallel", "arbitrary" if fused is not None else "parallel")),
        name="inproj",
    )(*ins)
    return {o[0]: r for o, r in zip(outs, res)}


def _forget_block(logf, ltri_ref, ones_ref, place_ref, row_ref, thr_ref, qa_o, ka_o, base_o, skip_o, carry, bases):
    i = pl.program_id(1)

    @pl.when(i == 0)
    def _():
        carry[...] = jnp.zeros_like(carry)
        bases[...] = jnp.zeros_like(bases)

    ltri = ltri_ref[...]
    w_aug = qa_o.shape[2]
    run = jnp.zeros((1, LANE), F32)
    for r in range(logf.shape[0] // LANE):
        rows = slice(r * LANE, (r + 1) * LANE)
        parts = _dot(ltri, jnp.concatenate(_split3(logf[rows]), axis=1))
        c = parts[:, :LANE] + parts[:, LANE:2 * LANE] + parts[:, 2 * LANE:] + run
        run = c[LANE - 1:LANE, :]
        placed = _dot(jnp.concatenate(_split3(c), axis=1), place_ref[...]) + row_ref[...]
        qa_o[0, rows, :] = placed[:, :w_aug].astype(BF16)
        ka_o[0, rows, :] = placed[:, w_aug:].astype(BF16)

    base = carry[...]
    rr = lax.broadcasted_iota(jnp.int32, (2 * SUBLANE, LANE), 0)
    cc = lax.broadcasted_iota(jnp.int32, (2 * SUBLANE, LANE), 1)
    diag = jnp.where(rr == cc, jnp.broadcast_to(base, (2 * SUBLANE, LANE)), 0.0)
    base_o[0, 0] = _dot3_lhs(diag, ones_ref[...])[:SUBLANE]
    carry[...] = base + run

    bases[pl.ds(i, 1), :] = base
    jj = lax.broadcasted_iota(jnp.int32, bases.shape, 0)
    far = (jj >= 1) & (jj <= i) & (bases[...] - base > thr_ref[...])
    skip_o[0, 0] = jnp.broadcast_to(jnp.sum(jnp.where(far, 1.0, 0.0), axis=0, keepdims=True), (SUBLANE, LANE))


def _forget_tables():
    w = N_PAIR * LANE
    pq, pk = np.zeros((3, LANE, w), np.float32), np.zeros((3, LANE, w), np.float32)
    qrow, krow = np.zeros((1, w), np.float32), np.zeros((1, w), np.float32)
    for h in range(H_F):
        l0 = (h // 2) * LANE + (h % 2) * DH_F
        for s in range(3):
            pq[s, h, l0 + s] = 1.0
            pk[s, h, l0 + 3 + s] = -1.0
            qrow[0, l0 + 3 + s] = 1.0
            krow[0, l0 + s] = 1.0
    r = np.arange(LANE)
    ltri = (r[None, :] <= r[:, None]).astype(np.float32)
    as_bf16 = lambda a: jnp.asarray(a, dtype=BF16)
    place = np.concatenate([pq.reshape(3 * LANE, w), pk.reshape(3 * LANE, w)], axis=1)
    return (as_bf16(ltri), jnp.ones((LANE, LANE), BF16), as_bf16(place),
            jnp.asarray(np.concatenate([qrow, krow], axis=1)))


def _fox_prompt_body(skip_ref, q_ref, qa_ref, k_ref, ka_ref, v_ref, base_ref, o_ref, m_scr, l_scr, acc_scr):
    b = pl.program_id(0)
    hp = pl.program_id(1)
    i = pl.program_id(2)
    tq = q_ref.shape[1]
    tk = tq
    lane2 = lax.broadcasted_iota(jnp.int32, (1, 2 * LANE), 1)
    first = (lane2 % LANE) < DH_F
    qf = jnp.concatenate([q_ref[0], qa_ref[0]], axis=1)
    zero = jnp.zeros_like(qf)
    q_heads = (jnp.where(first, qf, zero), jnp.where(first, zero, qf))

    m_scr[...] = jnp.full(m_scr.shape, NEG_INF, F32)
    l_scr[...] = jnp.zeros_like(l_scr)
    acc_scr[...] = jnp.zeros_like(acc_scr)

    def step(hh, j, masked):
        off = pl.multiple_of(j * tk, tk)
        kf = jnp.concatenate([k_ref[0, pl.ds(off, tk), :], ka_ref[0, pl.ds(off, tk), :]], axis=1)
        s = _dot_nt(q_heads[hh], kf)
        if masked:
            row = lax.broadcasted_iota(jnp.int32, (tq, tk), 0)
            col = lax.broadcasted_iota(jnp.int32, (tq, tk), 1)
            s = jnp.where(col <= row, s, NEG_INF)
        h = 2 * hp + hh
        d = base_ref[0, i, pl.ds(h, 1), :] - base_ref[0, j, pl.ds(h, 1), :]
        m_prev = m_scr[hh]
        m_next = jnp.maximum(m_prev, jnp.max(s, axis=1, keepdims=True) + d)
        alpha = jnp.exp(m_prev - m_next)
        sub = m_next - d
        p = jnp.exp(s - jnp.concatenate([sub] * (tk // LANE), axis=1))
        l_scr[hh] = alpha * l_scr[hh] + jnp.sum(p, axis=1, keepdims=True)
        acc_scr[hh] = alpha * acc_scr[hh] + _dot(p.astype(BF16), v_ref[0, pl.ds(off, tk), :])
        m_scr[hh] = m_next

    entry = (b * pl.num_programs(2) + i) * H_F + 2 * hp
    j0 = jnp.minimum(skip_ref[entry], skip_ref[entry + 1])

    def full_step(j, carry):
        step(0, j, False)
        step(1, j, False)
        return carry

    lax.fori_loop(j0, i, full_step, 0)
    step(0, i, True)
    step(1, i, True)

    lane = lax.broadcasted_iota(jnp.int32, (1, LANE), 1)
    o = jnp.where(lane < DH_F, acc_scr[0] / l_scr[0], acc_scr[1] / l_scr[1])
    o_ref[0] = o.astype(o_ref.dtype)


def _fox_prompt(skip, fq, qa, fkb, ka, fvb, base, blk):
    nb, nt, _ = fq.shape
    nblk = nt // blk
    qspec = pl.BlockSpec((1, blk, LANE), lambda b, p, i, sk: (b, i, p))
    kspec = pl.BlockSpec((1, nt, LANE), lambda b, p, i, sk: (b, 0, p))
    grid_spec = pltpu.PrefetchScalarGridSpec(
        num_scalar_prefetch=1,
        grid=(nb, N_PAIR, nblk),
        in_specs=[qspec, qspec, kspec, kspec, kspec,
                  pl.BlockSpec((1, nblk, SUBLANE, LANE), lambda b, p, i, sk: (b, 0, 0, 0))],
        out_specs=qspec,
        scratch_shapes=[pltpu.VMEM((2, blk, LANE), F32)] * 3,
    )
    return pl.pallas_call(
        _fox_prompt_body,
        out_shape=jax.ShapeDtypeStruct((nb, nt, D_FQK), BF16),
        grid_spec=grid_spec,
        compiler_params=_cparams(("parallel", "parallel", "parallel")),
        name="fox_prompt",
    )(skip, fq, qa, fkb, ka, fvb, base)


def _log_gamma(h):
    return math.log(1.0 - 2.0 ** (-5.0 - h))


def _group_norm_gate(o, w, rg):
    mu = jnp.mean(o, axis=-1, keepdims=True)
    var = jnp.mean(jnp.square(o - mu), axis=-1, keepdims=True)
    on = (o - mu) * lax.rsqrt(var + EPS)
    rg = rg.astype(F32)
    return (rg * jax.nn.sigmoid(rg)) * (on * w)


def _retention_block(q_heads, k_heads, v_all, g_all, w_ref, o_ref, st_ref, s_scr):
    t = pl.program_id(1)

    @pl.when(t == 0)
    def _():
        s_scr[...] = jnp.zeros_like(s_scr)

    L = RET_CHUNK
    ri = lax.broadcasted_iota(jnp.int32, (L, L), 0).astype(F32)
    ci = lax.broadcasted_iota(jnp.int32, (L, L), 1).astype(F32)
    diff = ri - ci
    for c in range(v_all.shape[0] // L):
        rows = slice(c * L, (c + 1) * L)
        for h in range(H_R):
            lg = _log_gamma(h)
            vc = slice(h * DV_R, (h + 1) * DV_R)
            q = q_heads[h][rows]
            k = k_heads[h][rows]
            v = v_all[rows, vc]
            decay = jnp.where(diff >= 0, jnp.exp(lg * jnp.maximum(diff, 0.0)), 0.0)
            scores = _dot_nt(q, k) * decay
            s0 = s_scr[h]
            qd = (q.astype(F32) * jnp.exp(lg * (ri + 1.0))).astype(BF16)
            o = _dot(jnp.concatenate([scores.astype(BF16), qd], axis=1),
                     jnp.concatenate([v, s0.astype(BF16)], axis=0))
            kd = (k.astype(F32) * jnp.exp(lg * (L - 1.0 - ri))).astype(BF16)
            s_scr[h] = s0 * math.exp(lg * L) + _dot_tn(kd, v)
            o_ref[0, rows, vc] = _group_norm_gate(o, w_ref[:, vc], g_all[rows, vc]).astype(o_ref.dtype)

    @pl.when(t == pl.num_programs(1) - 1)
    def _():
        st_ref[0] = s_scr[...]


def _ret_decode_body(q_ref, k_ref, v_ref, g_ref, w_ref, s_ref, o_ref, so_ref):
    rr = lax.broadcasted_iota(jnp.int32, (DK_R, DK_R), 0)
    cc = lax.broadcasted_iota(jnp.int32, (DK_R, DK_R), 1)
    eye = rr == cc
    for n in range(q_ref.shape[0]):
        for h in range(H_R):
            gamma = math.exp(_log_gamma(h))
            kc = slice(h * DK_R, (h + 1) * DK_R)
            vc = slice(h * DV_R, (h + 1) * DV_R)
            q = q_ref[n, :, kc]
            k = k_ref[n, :, kc]
            v = v_ref[n, :, vc]
            s0 = s_ref[n, h]
            qk = jnp.sum(q.astype(F32) * k.astype(F32), axis=-1, keepdims=True)
            q_s = _dot(jnp.broadcast_to(q, (2 * SUBLANE, DK_R)), s0.astype(BF16))[:1]
            o = qk.astype(BF16).astype(F32) * v.astype(F32) + q_s * gamma
            k_diag = jnp.where(eye, jnp.broadcast_to(k.astype(F32), (DK_R, DK_R)), 0.0).astype(BF16)
            v_rows = jnp.broadcast_to(v.astype(F32), (DK_R, DV_R)).astype(BF16)
            so_ref[n, h] = s0 * gamma + _dot(k_diag, v_rows)
            o_ref[n, :, vc] = _group_norm_gate(o, w_ref[:, vc], g_ref[n, :, vc]).astype(o_ref.dtype)


def _ret_decode(rq, rk, rv, rg, norm_w, state):
    ns = rq.shape[0]
    sb = min(DEC_SEQS, ns)
    spec = lambda w: pl.BlockSpec((sb, 1, w), lambda i: (i, 0, 0))
    st_spec = pl.BlockSpec((sb, H_R, DK_R, DV_R), lambda i: (i, 0, 0, 0))
    return pl.pallas_call(
        _ret_decode_body,
        out_shape=[jax.ShapeDtypeStruct((ns, 1, D_RV), BF16), jax.ShapeDtypeStruct(state.shape, F32)],
        grid=(ns // sb,),
        in_specs=[spec(D_RQK), spec(D_RQK), spec(D_RV), spec(D_RV), _const_spec((1, D_RV)), st_spec],
        out_specs=[spec(D_RV), st_spec],
        compiler_params=_cparams(("parallel",)),
        name="ret_decode",
    )(rq, rk, rv, rg, norm_w, state)


def _page_totals_body(lf_ref, ones_ref, o_ref):
    ones = ones_ref[...]
    for c in range(o_ref.shape[0]):
        hi, mid, lo = _split3(lf_ref[c * LANE:(c + 1) * LANE, :])
        o_ref[c:c + 1, :] = (_dot_nt(ones, hi) + _dot_nt(ones, mid) + _dot_nt(ones, lo))[:1]


def _page_totals(clf_t):
    n_phys = clf_t.shape[0]
    rows = clf_t.reshape(n_phys * H_F, PAGE_SIZE)
    blk = SUBLANE * LANE
    n_rows = -(-rows.shape[0] // blk) * blk
    if n_rows != rows.shape[0]:
        rows = jnp.pad(rows, ((0, n_rows - rows.shape[0]), (0, 0)))
    step = max(d for d in range(blk, 8 * blk + 1, blk) if n_rows % d == 0)
    tot = pl.pallas_call(
        _page_totals_body,
        out_shape=jax.ShapeDtypeStruct((n_rows // LANE, LANE), F32),
        grid=(n_rows // step,),
        in_specs=[pl.BlockSpec((step, PAGE_SIZE), lambda i: (i, 0)), _const_spec((2 * SUBLANE, LANE))],
        out_specs=pl.BlockSpec((step // LANE, LANE), lambda i: (i, 0)),
        compiler_params=_cparams(("parallel",)),
        name="page_totals",
    )(rows, jnp.ones((2 * SUBLANE, LANE), BF16))
    return tot.reshape(-1)[:n_phys * H_F].reshape(n_phys, H_F)


def _fox_decode_body(pt_ref, ng_ref, q_ref, kn_ref, vn_ref, lfn_ref, msuf_ref, ones_ref, pex_ref,
                     ck_hbm, cv_hbm, clf_hbm, o_ref,
                     kbuf, vbuf, lfbuf, sem, m_run, l_run, tail, acc_run, qrep, count, *, n_pages, n_tab):
    s = pl.program_id(0)
    ng = ng_ref[s]
    stat = (SUBLANE, LANE)

    def page_copies(seq, group, slot):
        copies = []
        for r in range(n_pages):
            page = pt_ref[seq, n_tab - 1 - (group * n_pages + r)]
            for kind, (hbm, buf) in enumerate(((ck_hbm, kbuf), (cv_hbm, vbuf), (clf_hbm, lfbuf))):
                copies.append(pltpu.make_async_copy(hbm.at[page], buf.at[slot, r], sem.at[slot, kind]))
        return copies

    def head_sums(x):
        return x.reshape(H_F, DH_F // SUBLANE, SUBLANE, LANE).sum(axis=1).sum(axis=1)

    @pl.when(s == 0)
    def _():
        count[0] = 0
        for c in page_copies(0, 0, 0):
            c.start()

    def column(ref):
        x = ref[0]
        mine = lax.broadcasted_iota(jnp.int32, x.shape, 1) == s
        return jnp.sum(jnp.where(mine, x, 0.0), axis=1, keepdims=True)

    first = count[0]
    q = jnp.broadcast_to(column(q_ref), (D_FQK, LANE))
    qrep[...] = q
    s_self = head_sums(q * column(kn_ref))
    v_new = column(vn_ref)
    m_run[...] = s_self
    l_run[...] = jnp.zeros(stat, F32)
    acc_run[...] = jnp.zeros_like(acc_run)
    tail[...] = jnp.broadcast_to(column(lfn_ref), stat)

    def group_step(g, carry):
        slot = lax.rem(first + g, 2)

        @pl.when(g + 1 < ng)
        def _():
            for c in page_copies(s, g + 1, 1 - slot):
                c.start()

        @pl.when(jnp.logical_and(g + 1 == ng, s + 1 < pl.num_programs(0)))
        def _():
            for c in page_copies(s + 1, 0, 1 - slot):
                c.start()

        for c in page_copies(s, g, slot):
            c.wait()

        qv = qrep[...]
        rows = [lfbuf[slot, r] for r in range(n_pages)]
        rows.append(jnp.zeros((LANE - n_pages * SUBLANE, LANE), F32))
        lf = jnp.concatenate(rows, axis=0)
        suf = _dot3_lhs(lf, msuf_ref[...])
        tot = _dot3_lhs(lf, ones_ref[...])
        pre = _dot3_rhs(pex_ref[...], tot)
        t_in = tail[...]
        last = slice((n_pages - 1) * SUBLANE, n_pages * SUBLANE)
        tail[...] = t_in + pre[last] + tot[last]
        s_pages = []
        m_step = jnp.full((SUBLANE, 1), NEG_INF, F32)
        for r in range(n_pages):
            pr = slice(r * SUBLANE, (r + 1) * SUBLANE)
            sr = head_sums(kbuf[slot, r] * qv) + (suf[pr] + pre[pr] + t_in)
            s_pages.append(sr)
            m_step = jnp.maximum(m_step, jnp.max(sr, axis=1, keepdims=True))
        m_prev = m_run[...]
        m_new = jnp.maximum(m_prev, m_step)
        alpha = jnp.exp(m_prev - m_new)
        p_pages = [jnp.exp(sr - m_new) for sr in s_pages]
        l_step = jnp.zeros((SUBLANE, 1), F32)
        for p in p_pages:
            l_step = l_step + jnp.sum(p, axis=1, keepdims=True)
        for h in range(H_F):
            hr = slice(h * DH_F, (h + 1) * DH_F)
            acc = acc_run[hr, :] * alpha[h:h + 1, :]
            for r in range(n_pages):
                acc = acc + vbuf[slot, r, hr, :] * p_pages[r][h:h + 1, :]
            acc_run[hr, :] = acc
        l_run[...] = alpha * l_run[...] + l_step
        m_run[...] = m_new
        return carry

    lax.fori_loop(0, ng, group_step, 0)
    count[0] = first + ng

    w_self = jnp.exp(s_self - m_run[...])
    l_tot = l_run[...] + w_self
    for h in range(H_F):
        hr = slice(h * DH_F, (h + 1) * DH_F)
        acc_col = jnp.sum(acc_run[hr, :], axis=1, keepdims=True)
        o_ref[0, hr, :] = (acc_col + w_self[h:h + 1, :1] * v_new[hr, :]) / l_tot[h:h + 1, :1]


def _fox_decode(page_table, fq_t, fk_t, fv_t, lfn_t, ck_t, cv_t, clf_t, thr):
    ns, n_tab = page_table.shape
    pg = min(DEC_PAGES, n_tab)
    assert pg * SUBLANE <= LANE and n_tab % pg == 0

    tot = _page_totals(clf_t)[page_table]
    after = jnp.transpose(lfn_t[0])[:, None, :] + jnp.cumsum(tot[:, ::-1], axis=1)[:, ::-1] - tot
    need = jnp.maximum(jnp.sum(jnp.any(after >= -thr, axis=-1), axis=1), 1)
    n_groups = ((need + pg - 1) // pg).astype(jnp.int32)

    r = np.arange(LANE)
    msuf = jnp.asarray((r[:, None] > r[None, :]).astype(np.float32), dtype=BF16)
    same_head = (r[:, None] % SUBLANE) == (r[None, :] % SUBLANE)
    in_step = (r[:, None] < pg * SUBLANE) & (r[None, :] < pg * SUBLANE)
    pex = jnp.asarray((same_head & in_step & (r[None, :] // SUBLANE < r[:, None] // SUBLANE)).astype(np.float32),
                      dtype=BF16)
    hbm = pl.BlockSpec(memory_space=pl.ANY)
    stat = pltpu.VMEM((SUBLANE, LANE), F32)
    grid_spec = pltpu.PrefetchScalarGridSpec(
        num_scalar_prefetch=2,
        grid=(ns,),
        in_specs=[_const_spec((1, D_FQK, ns)), _const_spec((1, D_FQK, ns)), _const_spec((1, D_FQK, ns)),
                  _const_spec((1, H_F, ns)), _const_spec((LANE, LANE)),
                  _const_spec((LANE, LANE)), _const_spec((LANE, LANE)), hbm, hbm, hbm],
        out_specs=pl.BlockSpec((1, D_FQK, 1), lambda s, pt, ng: (s, 0, 0)),
        scratch_shapes=[pltpu.VMEM((2, pg, D_FQK, PAGE_SIZE), F32), pltpu.VMEM((2, pg, D_FQK, PAGE_SIZE), F32),
                        pltpu.VMEM((2, pg, H_F, PAGE_SIZE), F32), pltpu.SemaphoreType.DMA((2, 3)),
                        stat, stat, stat, pltpu.VMEM((D_FQK, LANE), F32), pltpu.VMEM((D_FQK, LANE), F32),
                        pltpu.SMEM((1,), jnp.int32)],
    )
    return pl.pallas_call(
        functools.partial(_fox_decode_body, n_pages=pg, n_tab=n_tab),
        out_shape=jax.ShapeDtypeStruct((ns, D_FQK, 1), F32),
        grid_spec=grid_spec,
        compiler_params=_cparams(("arbitrary",)),
        name="fox_decode",
    )(page_table, n_groups, fq_t, fk_t, fv_t, lfn_t, msuf, jnp.ones((LANE, LANE), BF16), pex,
      ck_t, cv_t, clf_t)


def _outproj_body(x_ref, gt_ref, or_ref, of_ref, gr_ref, gf_ref, wb_ref, wo_ref, o_ref, *, per_row):
    gt = _mod_rows(gt_ref, per_row)
    merged = (gr_ref[0].astype(F32) * _dot(or_ref[0], wb_ref[:D_RV, :])
              + gf_ref[0].astype(F32) * _dot(of_ref[0], wb_ref[D_RV:, :]))
    y = _dot(merged.astype(BF16), wo_ref[...])
    o_ref[0] = x_ref[0] + gt * y


def _outproj(x, mod, per_row, row_off, o_r, o_f, gr, gf, w_branch, w_out):
    nb, nt, _ = x.shape
    tm = min(FFN_TILE, nt)
    row = lambda w: pl.BlockSpec((1, tm, w), lambda b, t: (b, t, 0))
    return pl.pallas_call(
        functools.partial(_outproj_body, per_row=per_row),
        out_shape=jax.ShapeDtypeStruct(x.shape, F32),
        grid=(nb, nt // tm),
        in_specs=[row(D_MODEL), _mod_specs(1, per_row, nt, row_off)[2], row(D_RV), row(D_FQK),
                  row(D_MODEL), row(D_MODEL), _const_spec((D_RV + D_FQK, D_MODEL)),
                  _const_spec((D_MODEL, D_MODEL))],
        out_specs=row(D_MODEL),
        compiler_params=_cparams(("parallel", "parallel")),
        name="outproj",
    )(x, mod, o_r, o_f, gr, gf, w_branch, w_out)


def _rope_tables(pos):
    half = DK_R // 2
    inv = ROPE_BASE ** (-jnp.arange(half, dtype=F32) / half)
    ang = pos.astype(F32)[:, None] * inv[None, :]
    cos, sin = jnp.cos(ang), jnp.sin(ang)
    return jnp.concatenate([cos, cos], axis=1), jnp.concatenate([-sin, sin], axis=1)


def _permute_w_in(w_in):
    sizes = (D_RQK, D_RQK, D_RV, D_RV, D_FQK, D_FQK, D_FQK, H_F, D_MODEL, D_MODEL)
    offs = [0]
    for s in sizes:
        offs.append(offs[-1] + s)
    part = lambda i: w_in[:, offs[i]:offs[i + 1]]
    fl = jnp.pad(part(7), ((0, 0), (0, LANE - H_F)))
    return jnp.concatenate([part(i) for i in (0, 1, 2, 3, 4, 5, 6, 8, 9)] + [fl], axis=1).astype(BF16)


def kernel(x_prompt, x_sample, cache_k, cache_v, cache_logf, state_ret, page_table, c_prompt, c_sample,
           w_ada, b_ada, w_ffa_up, w_ffa_down, w_in, b_forget, q_norm_w, k_norm_w, ret_norm_w,
           w_branch, w_out, w_ffb_up, w_ffb_down):
    nb, nt, _ = x_prompt.shape
    ns = x_sample.shape[0]
    depth = w_ada.shape[0]
    past_len = page_table.shape[1] * PAGE_SIZE
    assert x_sample.shape[1] == 1 and nb <= SUBLANE and ns % SUBLANE == 0 and nt % ATT_BLK == 0

    pad = (-(ns + SUBLANE)) % (2 * SUBLANE)
    c_all = jnp.concatenate([c_sample, c_prompt, jnp.zeros((SUBLANE - nb + pad, D_MODEL), F32)], axis=0)
    cos_p, sin_p = _rope_tables(jnp.arange(nt, dtype=jnp.int32))
    cos_s, sin_s = _rope_tables(jnp.full((ns,), past_len, jnp.int32))
    hd = np.arange(D_FQK) // DH_F
    bd = jnp.asarray((hd[:, None] == hd[None, :]).astype(np.float32) / DH_F, dtype=BF16)

    y_p = x_prompt
    y_s = x_sample.reshape(1, ns, D_MODEL)
    outs = [[] for _ in range(8)]
    for l in range(depth):
        bf = lambda w: w[l].astype(BF16)
        w_up_a, w_dn_a, w_up_b, w_dn_b = bf(w_ffa_up), bf(w_ffa_down), bf(w_ffb_up), bf(w_ffb_down)
        w_br, w_o = bf(w_branch), bf(w_out)
        w_inp = _permute_w_in(w_in[l])
        bf_row = jnp.pad(b_forget[l], (0, LANE - H_F)).reshape(1, LANE)
        qn = jnp.tile(q_norm_w[l], H_F).reshape(1, D_FQK)
        kn = jnp.tile(k_norm_w[l], H_F).reshape(1, D_FQK)
        rn = ret_norm_w[l].reshape(1, D_RV)
        mod = _adaln(c_all, w_ada[l], b_ada[l])
        qk_bound = 1.02 * DH_F ** 0.5 * jnp.max(jnp.abs(q_norm_w[l])) * jnp.max(jnp.abs(k_norm_w[l]))
        thr = EXP_UNDERFLOW + 2.0 * qk_bound
        page_view = lambda c, w: jnp.transpose(c, (0, 2, 3, 1)).reshape(c.shape[0], w, PAGE_SIZE)

        y_p = _ffn(y_p, mod, 0, False, ns, w_up_a, w_dn_a)
        pj = _inproj(y_p, mod, False, ns, cos_p, sin_p, w_inp, bf_row, qn, kn, bd, fused=(rn, thr))
        skip = pj["skip"][:, :, 0, :H_F].astype(jnp.int32).reshape(-1)
        o_f = _fox_prompt(skip, pj["fq"], pj["qa"], pj["fkb"], pj["ka"], pj["fvb"], pj["base"], ATT_BLK)
        st_p = pj["st"]
        y_p = _outproj(y_p, mod, False, ns, pj["o_r"], o_f, pj["gr"], pj["gf"], w_br, w_o)
        y_p = _ffn(y_p, mod, 2, False, ns, w_up_b, w_dn_b)

        y_s = _ffn(y_s, mod, 0, True, 0, w_up_a, w_dn_a)
        sj = _inproj(y_s, mod, True, 0, cos_s, sin_s, w_inp, bf_row, qn, kn, bd)
        seq = lambda a: a.reshape(ns, 1, a.shape[-1])
        o_f_s = _fox_decode(page_table, sj["fq"], sj["fkt"], sj["fvt"], sj["lft"],
                            page_view(cache_k[l], D_FQK), page_view(cache_v[l], D_FQK),
                            jnp.transpose(cache_logf[l], (0, 2, 1)), thr)
        o_r_s, st_s = _ret_decode(seq(sj["rq"]), seq(sj["rk"]), seq(sj["rv"]), seq(sj["rg"]), rn, state_ret[l])
        y_s = _outproj(y_s, mod, True, 0, o_r_s.reshape(1, ns, D_RV), o_f_s.reshape(1, ns, D_FQK).astype(BF16),
                       sj["gr"], sj["gf"], w_br, w_o)
        y_s = _ffn(y_s, mod, 2, True, 0, w_up_b, w_dn_b)

        rows = lambda a: jnp.transpose(a.reshape(a.shape[0], H_F, DH_F, a.shape[2]), (0, 3, 1, 2))
        for dst, val in zip(outs, (rows(pj["fkt"]), rows(pj["fvt"]), jnp.transpose(pj["lft"], (0, 2, 1)), st_p,
                                   rows(sj["fkt"]).reshape(ns, 1, H_F, DH_F),
                                   rows(sj["fvt"]).reshape(ns, 1, H_F, DH_F),
                                   jnp.transpose(sj["lft"], (0, 2, 1)).reshape(ns, 1, H_F), st_s)):
            dst.append(val)

    return (y_p, y_s.reshape(ns, 1, D_MODEL), *[jnp.stack(o) for o in outs])
```

```python
import functools
import math

import jax
import jax.numpy as jnp
import numpy as np
from jax import lax
from jax.experimental import pallas as pl
from jax.experimental.pallas import tpu as pltpu

D_MODEL = 1024
PAGE_SIZE = 128
H_R, DK_R, DV_R = 4, 128, 256
RET_CHUNK = 128
ROPE_BASE = 10000.0
H_F, DH_F = 8, 64
D_FF = 2816
EPS = 1e-6
N_SUB = 3
NEG_INF = -1e30
EXP_UNDERFLOW = 106.0

D_RQK = H_R * DK_R
D_RV = H_R * DV_R
D_FQK = H_F * DH_F
N_PAIR = H_F // 2
LANE = 128
SUBLANE = 8

F32 = jnp.float32
BF16 = jnp.bfloat16

_SEG = {}
_off = 0
for _name, _w in (("rq", D_RQK), ("rk", D_RQK), ("rv", D_RV), ("rg", D_RV), ("fq", D_FQK), ("fk", D_FQK),
                  ("fv", D_FQK), ("gr", D_MODEL), ("gf", D_MODEL), ("fl", LANE)):
    _SEG[_name] = (_off, _off + _w)
    _off += _w
D_IN_PAD = _off

ROW_TILE = 512
ATT_BLK = 512
FFN_TILE = 1024
FF_CHUNK = 256
RET_BLOCK = 512
DEC_PAGES = 8
DEC_SEQS = 4
VMEM_LIMIT = 56 * 2 ** 20


def _cparams(sem):
    return pltpu.CompilerParams(dimension_semantics=sem, vmem_limit_bytes=VMEM_LIMIT)


def _const_spec(shape):
    nd = len(shape)
    return pl.BlockSpec(shape, lambda *_: (0,) * nd, pipeline_mode=pl.Buffered(1))


def _dot(a, b):
    return jnp.dot(a, b, preferred_element_type=F32)


def _dot_nt(a, b):
    return lax.dot_general(a, b, (((1,), (1,)), ((), ())), preferred_element_type=F32)


def _dot_tn(a, b):
    return lax.dot_general(a, b, (((0,), (0,)), ((), ())), preferred_element_type=F32)


def _split3(x):
    hi = x.astype(BF16)
    r1 = x - hi.astype(F32)
    mid = r1.astype(BF16)
    lo = (r1 - mid.astype(F32)).astype(BF16)
    return hi, mid, lo


def _dot3_rhs(a_bf16, x):
    hi, mid, lo = _split3(x)
    return _dot(a_bf16, hi) + _dot(a_bf16, mid) + _dot(a_bf16, lo)


def _dot3_lhs(x, b_bf16):
    hi, mid, lo = _split3(x)
    return _dot(hi, b_bf16) + _dot(mid, b_bf16) + _dot(lo, b_bf16)


def _log_sigmoid(x):
    return jnp.minimum(x, 0.0) - jnp.log1p(jnp.exp(-jnp.abs(x)))


def _mod_rows(ref, per_row):
    if per_row:
        return ref[0]
    return ref[0, pl.ds(pl.program_id(0), 1), :]


def _modulated_norm(x, shift, scale):
    ms = jnp.mean(x * x, axis=-1, keepdims=True)
    return (x * lax.rsqrt(ms + EPS)) * (1.0 + scale) + shift


def _mod_specs(sub, per_row, n_rows, row_off):
    specs = []
    for k in range(3):
        j = sub * 3 + k
        if per_row:
            specs.append(pl.BlockSpec((1, n_rows, D_MODEL), lambda b, t, j=j: (j, 0, 0)))
        else:
            specs.append(pl.BlockSpec((1, SUBLANE, D_MODEL), lambda b, t, j=j: (j, row_off // SUBLANE, 0)))
    return specs


def _adaln_body(c_ref, w_ref, b_ref, o_ref):
    c = c_ref[...]
    s = c * jax.nn.sigmoid(c)
    o_ref[0] = _dot(s.astype(BF16), w_ref[...].astype(BF16)) + b_ref[...]


def _adaln(c_all, w_ada, b_ada):
    rows = c_all.shape[0]
    n = N_SUB * 3
    return pl.pallas_call(
        _adaln_body,
        out_shape=jax.ShapeDtypeStruct((n, rows, D_MODEL), F32),
        grid=(n,),
        in_specs=[pl.BlockSpec((rows, D_MODEL), lambda j: (0, 0)),
                  pl.BlockSpec((D_MODEL, D_MODEL), lambda j: (0, j)),
                  pl.BlockSpec((1, D_MODEL), lambda j: (0, j))],
        out_specs=pl.BlockSpec((1, rows, D_MODEL), lambda j: (j, 0, 0)),
        compiler_params=_cparams(("parallel",)),
        name="adaln",
    )(c_all, w_ada, b_ada.reshape(1, -1))


def _ffn_body(x_ref, sh_ref, sc_ref, gt_ref, wup_ref, wdn_ref, o_ref, *, per_row):
    x = x_ref[0]
    sh, sc, gt = (_mod_rows(r, per_row) for r in (sh_ref, sc_ref, gt_ref))
    h = _modulated_norm(x, sh, sc).astype(BF16)
    acc = None
    for c in range(D_FF // FF_CHUNK):
        lo = c * FF_CHUNK
        g = _dot(h, wup_ref[:, lo:lo + FF_CHUNK])
        u = _dot(h, wup_ref[:, D_FF + lo:D_FF + lo + FF_CHUNK])
        a = (g * jax.nn.sigmoid(g) * u).astype(BF16)
        part = _dot(a, wdn_ref[lo:lo + FF_CHUNK, :])
        acc = part if acc is None else acc + part
    o_ref[0] = x + (0.5 * gt) * acc


def _ffn(x, mod, sub, per_row, row_off, w_up, w_dn):
    nb, nt, _ = x.shape
    tm = min(FFN_TILE, nt)
    x_spec = pl.BlockSpec((1, tm, D_MODEL), lambda b, t: (b, t, 0))
    return pl.pallas_call(
        functools.partial(_ffn_body, per_row=per_row),
        out_shape=jax.ShapeDtypeStruct(x.shape, F32),
        grid=(nb, nt // tm),
        in_specs=[x_spec, *_mod_specs(sub, per_row, nt, row_off),
                  _const_spec((D_MODEL, 2 * D_FF)), _const_spec((D_FF, D_MODEL))],
        out_specs=x_spec,
        compiler_params=_cparams(("parallel", "parallel")),
        name="ffn",
    )(x, mod, mod, mod, w_up, w_dn)


def _inproj_body(*refs, per_row, fused):
    (x_ref, sh_ref, sc_ref, cos_ref, sin_ref, w_ref, bf_ref, qn_ref, kn_ref, bd_ref), refs = refs[:10], refs[10:]
    if fused:
        (rn_ref, ltri_ref, ones_ref, place_ref, row_ref, thr_ref), refs = refs[:6], refs[6:]
        (or_o, st_o, qa_o, ka_o, base_o, skip_o), refs = refs[:6], refs[6:]
    else:
        (rq_o, rk_o, rv_o, rg_o), refs = refs[:4], refs[4:]
    (fq_o, fkb_o, fvb_o, fkt_o, fvt_o, gr_o, gf_o, lft_o), refs = refs[:8], refs[8:]

    x = x_ref[0]
    sh, sc = (_mod_rows(r, per_row) for r in (sh_ref, sc_ref))
    h = _modulated_norm(x, sh, sc).astype(BF16)

    def seg(name):
        lo, hi = _SEG[name]
        return _dot(h, w_ref[:, lo:hi])

    cos, sin = cos_ref[...], sin_ref[...]

    def rope_heads(z, scale):
        heads = []
        for hd in range(H_R):
            zh = z[:, hd * DK_R:(hd + 1) * DK_R]
            r = zh * cos + pltpu.roll(zh, DK_R // 2, 1) * sin
            heads.append((r * scale if scale != 1.0 else r).astype(BF16))
        return heads

    q_heads = rope_heads(seg("rq"), 1.0)
    k_heads = rope_heads(seg("rk"), DK_R ** -0.5)
    rv = seg("rv").astype(BF16)
    rg = seg("rg")
    if fused:
        s_scr, carry, bases = refs
        _retention_block(q_heads, k_heads, rv, rg, rn_ref, or_o, st_o, s_scr)
    else:
        for hd in range(H_R):
            rq_o[0, :, hd * DK_R:(hd + 1) * DK_R] = q_heads[hd]
            rk_o[0, :, hd * DK_R:(hd + 1) * DK_R] = k_heads[hd]
        rv_o[0] = rv
        rg_o[0] = rg.astype(BF16)

    bd = bd_ref[...]

    def head_rms(z, w):
        ms = _dot((z * z).astype(BF16), bd)
        return (z * lax.rsqrt(ms + EPS)) * w

    fq = (head_rms(seg("fq"), qn_ref[...]) * (DH_F ** -0.5)).astype(BF16)
    fq_o[0] = fq if fused else fq.astype(F32).T
    fk = head_rms(seg("fk"), kn_ref[...])
    fkt_o[0] = fk.T
    fkb_o[0] = fk.astype(BF16)
    fv = seg("fv")
    fvt_o[0] = fv.T
    fvb_o[0] = fv.astype(BF16)
    gr_o[0] = jax.nn.sigmoid(seg("gr")).astype(BF16)
    gf_o[0] = jax.nn.sigmoid(seg("gf")).astype(BF16)

    logf = _log_sigmoid(seg("fl") + bf_ref[...])
    lane = lax.broadcasted_iota(jnp.int32, logf.shape, 1)
    logf = jnp.where(lane < H_F, logf, 0.0)
    lft_o[0] = logf.T[:H_F]
    if fused:
        _forget_block(logf, ltri_ref, ones_ref, place_ref, row_ref, thr_ref, qa_o, ka_o, base_o, skip_o, carry, bases)


def _inproj(x, mod, per_row, row_off, cos_t, sin_t, w_in, bf_row, qn, kn, bd, fused=None):
    nb, nt, _ = x.shape
    tm = min(ROW_TILE, nt)
    nblk = nt // tm
    w_aug = N_PAIR * LANE
    row = lambda w: pl.BlockSpec((1, tm, w), lambda b, t: (b, t, 0))
    col = lambda w: pl.BlockSpec((1, w, tm), lambda b, t: (b, 0, t))
    rows = lambda name, w: (name, jax.ShapeDtypeStruct((nb, nt, w), BF16), row(w))
    cols = lambda name, w: (name, jax.ShapeDtypeStruct((nb, w, nt), F32), col(w))
    stat = lambda name: (name, jax.ShapeDtypeStruct((nb, nblk, SUBLANE, LANE), F32),
                         pl.BlockSpec((1, 1, SUBLANE, LANE), lambda b, t: (b, t, 0, 0)))
    ins = [x, mod, mod, cos_t, sin_t, w_in, bf_row, qn, kn, bd]
    in_specs = [row(D_MODEL), *_mod_specs(1, per_row, nt, row_off)[:2],
                pl.BlockSpec((tm, DK_R), lambda b, t: (t, 0)), pl.BlockSpec((tm, DK_R), lambda b, t: (t, 0)),
                _const_spec((D_MODEL, D_IN_PAD)), _const_spec((1, LANE)), _const_spec((1, D_FQK)),
                _const_spec((1, D_FQK)), _const_spec((D_FQK, D_FQK))]
    scratch = []
    if fused is not None:
        assert tm == ATT_BLK and tm % RET_CHUNK == 0
        norm_w, thr = fused
        tables = _forget_tables()
        ins += [norm_w, *tables, jnp.broadcast_to(thr.astype(F32), (1, LANE))]
        in_specs += [_const_spec(a.shape) for a in ins[10:]]
        outs = [rows("o_r", D_RV),
                ("st", jax.ShapeDtypeStruct((nb, H_R, DK_R, DV_R), F32),
                 pl.BlockSpec((1, H_R, DK_R, DV_R), lambda b, t: (b, 0, 0, 0))),
                rows("qa", w_aug), rows("ka", w_aug), stat("base"), stat("skip")]
        scratch = [pltpu.VMEM((H_R, DK_R, DV_R), F32), pltpu.VMEM((1, LANE), F32),
                   pltpu.VMEM((-(-nblk // SUBLANE) * SUBLANE, LANE), F32)]
    else:
        outs = [rows("rq", D_RQK), rows("rk", D_RQK), rows("rv", D_RV), rows("rg", D_RV)]
    outs += [(rows if fused is not None else cols)("fq", D_FQK),
             rows("fkb", D_FQK), rows("fvb", D_FQK), cols("fkt", D_FQK), cols("fvt", D_FQK),
             rows("gr", D_MODEL), rows("gf", D_MODEL), cols("lft", H_F)]
    res = pl.pallas_call(
        functools.partial(_inproj_body, per_row=per_row, fused=fused is not None),
        out_shape=[o[1] for o in outs],
        grid=(nb, nblk),
        in_specs=in_specs,
        out_specs=[o[2] for o in outs],
        scratch_shapes=scratch,
        compiler_params=_cparams(("parallel", "arbitrary" if fused is not None else "parallel")),
        name="inproj",
    )(*ins)
    return {o[0]: r for o, r in zip(outs, res)}


def _forget_block(logf, ltri_ref, ones_ref, place_ref, row_ref, thr_ref, qa_o, ka_o, base_o, skip_o, carry, bases):
    i = pl.program_id(1)

    @pl.when(i == 0)
    def _():
        carry[...] = jnp.zeros_like(carry)
        bases[...] = jnp.zeros_like(bases)

    ltri = ltri_ref[...]
    w_aug = qa_o.shape[2]
    run = jnp.zeros((1, LANE), F32)
    for r in range(logf.shape[0] // LANE):
        rows = slice(r * LANE, (r + 1) * LANE)
        parts = _dot(ltri, jnp.concatenate(_split3(logf[rows]), axis=1))
        c = parts[:, :LANE] + parts[:, LANE:2 * LANE] + parts[:, 2 * LANE:] + run
        run = c[LANE - 1:LANE, :]
        placed = _dot(jnp.concatenate(_split3(c), axis=1), place_ref[...]) + row_ref[...]
        qa_o[0, rows, :] = placed[:, :w_aug].astype(BF16)
        ka_o[0, rows, :] = placed[:, w_aug:].astype(BF16)

    base = carry[...]
    rr = lax.broadcasted_iota(jnp.int32, (2 * SUBLANE, LANE), 0)
    cc = lax.broadcasted_iota(jnp.int32, (2 * SUBLANE, LANE), 1)
    diag = jnp.where(rr == cc, jnp.broadcast_to(base, (2 * SUBLANE, LANE)), 0.0)
    base_o[0, 0] = _dot3_lhs(diag, ones_ref[...])[:SUBLANE]
    carry[...] = base + run

    bases[pl.ds(i, 1), :] = base
    jj = lax.broadcasted_iota(jnp.int32, bases.shape, 0)
    far = (jj >= 1) & (jj <= i) & (bases[...] - base > thr_ref[...])
    skip_o[0, 0] = jnp.broadcast_to(jnp.sum(jnp.where(far, 1.0, 0.0), axis=0, keepdims=True), (SUBLANE, LANE))


def _forget_tables():
    w = N_PAIR * LANE
    pq, pk = np.zeros((3, LANE, w), np.float32), np.zeros((3, LANE, w), np.float32)
    qrow, krow = np.zeros((1, w), np.float32), np.zeros((1, w), np.float32)
    for h in range(H_F):
        l0 = (h // 2) * LANE + (h % 2) * DH_F
        for s in range(3):
            pq[s, h, l0 + s] = 1.0
            pk[s, h, l0 + 3 + s] = -1.0
            qrow[0, l0 + 3 + s] = 1.0
            krow[0, l0 + s] = 1.0
    r = np.arange(LANE)
    ltri = (r[None, :] <= r[:, None]).astype(np.float32)
    as_bf16 = lambda a: jnp.asarray(a, dtype=BF16)
    place = np.concatenate([pq.reshape(3 * LANE, w), pk.reshape(3 * LANE, w)], axis=1)
    return (as_bf16(ltri), jnp.ones((LANE, LANE), BF16), as_bf16(place),
            jnp.asarray(np.concatenate([qrow, krow], axis=1)))


def _fox_prompt_body(skip_ref, q_ref, qa_ref, k_ref, ka_ref, v_ref, base_ref, o_ref, m_scr, l_scr, acc_scr):
    b = pl.program_id(0)
    hp = pl.program_id(1)
    i = pl.program_id(2)
    tq = q_ref.shape[1]
    tk = tq
    lane2 = lax.broadcasted_iota(jnp.int32, (1, 2 * LANE), 1)
    first = (lane2 % LANE) < DH_F
    qf = jnp.concatenate([q_ref[0], qa_ref[0]], axis=1)
    zero = jnp.zeros_like(qf)
    q_heads = (jnp.where(first, qf, zero), jnp.where(first, zero, qf))

    m_scr[...] = jnp.full(m_scr.shape, NEG_INF, F32)
    l_scr[...] = jnp.zeros_like(l_scr)
    acc_scr[...] = jnp.zeros_like(acc_scr)

    def step(hh, j, masked):
        off = pl.multiple_of(j * tk, tk)
        kf = jnp.concatenate([k_ref[0, pl.ds(off, tk), :], ka_ref[0, pl.ds(off, tk), :]], axis=1)
        s = _dot_nt(q_heads[hh], kf)
        if masked:
            row = lax.broadcasted_iota(jnp.int32, (tq, tk), 0)
            col = lax.broadcasted_iota(jnp.int32, (tq, tk), 1)
            s = jnp.where(col <= row, s, NEG_INF)
        h = 2 * hp + hh
        d = base_ref[0, i, pl.ds(h, 1), :] - base_ref[0, j, pl.ds(h, 1), :]
        m_prev = m_scr[hh]
        m_next = jnp.maximum(m_prev, jnp.max(s, axis=1, keepdims=True) + d)
        alpha = jnp.exp(m_prev - m_next)
        sub = m_next - d
        p = jnp.exp(s - jnp.concatenate([sub] * (tk // LANE), axis=1))
        l_scr[hh] = alpha * l_scr[hh] + jnp.sum(p, axis=1, keepdims=True)
        acc_scr[hh] = alpha * acc_scr[hh] + _dot(p.astype(BF16), v_ref[0, pl.ds(off, tk), :])
        m_scr[hh] = m_next

    entry = (b * pl.num_programs(2) + i) * H_F + 2 * hp
    j0 = jnp.minimum(skip_ref[entry], skip_ref[entry + 1])

    def full_step(j, carry):
        step(0, j, False)
        step(1, j, False)
        return carry

    lax.fori_loop(j0, i, full_step, 0)
    step(0, i, True)
    step(1, i, True)

    lane = lax.broadcasted_iota(jnp.int32, (1, LANE), 1)
    o = jnp.where(lane < DH_F, acc_scr[0] / l_scr[0], acc_scr[1] / l_scr[1])
    o_ref[0] = o.astype(o_ref.dtype)


def _fox_prompt(skip, fq, qa, fkb, ka, fvb, base, blk):
    nb, nt, _ = fq.shape
    nblk = nt // blk
    qspec = pl.BlockSpec((1, blk, LANE), lambda b, p, i, sk: (b, i, p))
    kspec = pl.BlockSpec((1, nt, LANE), lambda b, p, i, sk: (b, 0, p))
    grid_spec = pltpu.PrefetchScalarGridSpec(
        num_scalar_prefetch=1,
        grid=(nb, N_PAIR, nblk),
        in_specs=[qspec, qspec, kspec, kspec, kspec,
                  pl.BlockSpec((1, nblk, SUBLANE, LANE), lambda b, p, i, sk: (b, 0, 0, 0))],
        out_specs=qspec,
        scratch_shapes=[pltpu.VMEM((2, blk, LANE), F32)] * 3,
    )
    return pl.pallas_call(
        _fox_prompt_body,
        out_shape=jax.ShapeDtypeStruct((nb, nt, D_FQK), BF16),
        grid_spec=grid_spec,
        compiler_params=_cparams(("parallel", "parallel", "parallel")),
        name="fox_prompt",
    )(skip, fq, qa, fkb, ka, fvb, base)


def _log_gamma(h):
    return math.log(1.0 - 2.0 ** (-5.0 - h))


def _group_norm_gate(o, w, rg):
    mu = jnp.mean(o, axis=-1, keepdims=True)
    var = jnp.mean(jnp.square(o - mu), axis=-1, keepdims=True)
    on = (o - mu) * lax.rsqrt(var + EPS)
    rg = rg.astype(F32)
    return (rg * jax.nn.sigmoid(rg)) * (on * w)


def _retention_block(q_heads, k_heads, v_all, g_all, w_ref, o_ref, st_ref, s_scr):
    t = pl.program_id(1)

    @pl.when(t == 0)
    def _():
        s_scr[...] = jnp.zeros_like(s_scr)

    L = RET_CHUNK
    ri = lax.broadcasted_iota(jnp.int32, (L, L), 0).astype(F32)
    ci = lax.broadcasted_iota(jnp.int32, (L, L), 1).astype(F32)
    diff = ri - ci
    for c in range(v_all.shape[0] // L):
        rows = slice(c * L, (c + 1) * L)
        for h in range(H_R):
            lg = _log_gamma(h)
            vc = slice(h * DV_R, (h + 1) * DV_R)
            q = q_heads[h][rows]
            k = k_heads[h][rows]
            v = v_all[rows, vc]
            decay = jnp.where(diff >= 0, jnp.exp(lg * jnp.maximum(diff, 0.0)), 0.0)
            scores = _dot_nt(q, k) * decay
            s0 = s_scr[h]
            qd = (q.astype(F32) * jnp.exp(lg * (ri + 1.0))).astype(BF16)
            o = _dot(jnp.concatenate([scores.astype(BF16), qd], axis=1),
                     jnp.concatenate([v, s0.astype(BF16)], axis=0))
            kd = (k.astype(F32) * jnp.exp(lg * (L - 1.0 - ri))).astype(BF16)
            s_scr[h] = s0 * math.exp(lg * L) + _dot_tn(kd, v)
            o_ref[0, rows, vc] = _group_norm_gate(o, w_ref[:, vc], g_all[rows, vc]).astype(o_ref.dtype)

    @pl.when(t == pl.num_programs(1) - 1)
    def _():
        st_ref[0] = s_scr[...]


def _ret_decode_body(q_ref, k_ref, v_ref, g_ref, w_ref, s_ref, o_ref, so_ref):
    rr = lax.broadcasted_iota(jnp.int32, (DK_R, DK_R), 0)
    cc = lax.broadcasted_iota(jnp.int32, (DK_R, DK_R), 1)
    eye = rr == cc
    for n in range(q_ref.shape[0]):
        for h in range(H_R):
            gamma = math.exp(_log_gamma(h))
            kc = slice(h * DK_R, (h + 1) * DK_R)
            vc = slice(h * DV_R, (h + 1) * DV_R)
            q = q_ref[n, :, kc]
            k = k_ref[n, :, kc]
            v = v_ref[n, :, vc]
            s0 = s_ref[n, h]
            qk = jnp.sum(q.astype(F32) * k.astype(F32), axis=-1, keepdims=True)
            q_s = _dot(jnp.broadcast_to(q, (2 * SUBLANE, DK_R)), s0.astype(BF16))[:1]
            o = qk.astype(BF16).astype(F32) * v.astype(F32) + q_s * gamma
            k_diag = jnp.where(eye, jnp.broadcast_to(k.astype(F32), (DK_R, DK_R)), 0.0).astype(BF16)
            v_rows = jnp.broadcast_to(v.astype(F32), (DK_R, DV_R)).astype(BF16)
            so_ref[n, h] = s0 * gamma + _dot(k_diag, v_rows)
            o_ref[n, :, vc] = _group_norm_gate(o, w_ref[:, vc], g_ref[n, :, vc]).astype(o_ref.dtype)


def _ret_decode(rq, rk, rv, rg, norm_w, state):
    ns = rq.shape[0]
    sb = min(DEC_SEQS, ns)
    spec = lambda w: pl.BlockSpec((sb, 1, w), lambda i: (i, 0, 0))
    st_spec = pl.BlockSpec((sb, H_R, DK_R, DV_R), lambda i: (i, 0, 0, 0))
    return pl.pallas_call(
        _ret_decode_body,
        out_shape=[jax.ShapeDtypeStruct((ns, 1, D_RV), BF16), jax.ShapeDtypeStruct(state.shape, F32)],
        grid=(ns // sb,),
        in_specs=[spec(D_RQK), spec(D_RQK), spec(D_RV), spec(D_RV), _const_spec((1, D_RV)), st_spec],
        out_specs=[spec(D_RV), st_spec],
        compiler_params=_cparams(("parallel",)),
        name="ret_decode",
    )(rq, rk, rv, rg, norm_w, state)


def _page_totals_body(lf_ref, ones_ref, o_ref):
    ones = ones_ref[...]
    for c in range(o_ref.shape[0]):
        hi, mid, lo = _split3(lf_ref[c * LANE:(c + 1) * LANE, :])
        o_ref[c:c + 1, :] = (_dot_nt(ones, hi) + _dot_nt(ones, mid) + _dot_nt(ones, lo))[:1]


def _page_totals(clf_t):
    n_phys = clf_t.shape[0]
    rows = clf_t.reshape(n_phys * H_F, PAGE_SIZE)
    blk = SUBLANE * LANE
    n_rows = -(-rows.shape[0] // blk) * blk
    if n_rows != rows.shape[0]:
        rows = jnp.pad(rows, ((0, n_rows - rows.shape[0]), (0, 0)))
    step = max(d for d in range(blk, 8 * blk + 1, blk) if n_rows % d == 0)
    tot = pl.pallas_call(
        _page_totals_body,
        out_shape=jax.ShapeDtypeStruct((n_rows // LANE, LANE), F32),
        grid=(n_rows // step,),
        in_specs=[pl.BlockSpec((step, PAGE_SIZE), lambda i: (i, 0)), _const_spec((2 * SUBLANE, LANE))],
        out_specs=pl.BlockSpec((step // LANE, LANE), lambda i: (i, 0)),
        compiler_params=_cparams(("parallel",)),
        name="page_totals",
    )(rows, jnp.ones((2 * SUBLANE, LANE), BF16))
    return tot.reshape(-1)[:n_phys * H_F].reshape(n_phys, H_F)


def _fox_decode_body(pt_ref, ng_ref, q_ref, kn_ref, vn_ref, lfn_ref, msuf_ref, ones_ref, pex_ref,
                     ck_hbm, cv_hbm, clf_hbm, o_ref,
                     kbuf, vbuf, lfbuf, sem, m_run, l_run, tail, acc_run, qrep, count, *, n_pages, n_tab):
    s = pl.program_id(0)
    ng = ng_ref[s]
    stat = (SUBLANE, LANE)

    def page_copies(seq, group, slot):
        copies = []
        for r in range(n_pages):
            page = pt_ref[seq, n_tab - 1 - (group * n_pages + r)]
            for kind, (hbm, buf) in enumerate(((ck_hbm, kbuf), (cv_hbm, vbuf), (clf_hbm, lfbuf))):
                copies.append(pltpu.make_async_copy(hbm.at[page], buf.at[slot, r], sem.at[slot, kind]))
        return copies

    def head_sums(x):
        return x.reshape(H_F, DH_F // SUBLANE, SUBLANE, LANE).sum(axis=1).sum(axis=1)

    @pl.when(s == 0)
    def _():
        count[0] = 0
        for c in page_copies(0, 0, 0):
            c.start()

    def column(ref):
        x = ref[0]
        mine = lax.broadcasted_iota(jnp.int32, x.shape, 1) == s
        return jnp.sum(jnp.where(mine, x, 0.0), axis=1, keepdims=True)

    first = count[0]
    q = jnp.broadcast_to(column(q_ref), (D_FQK, LANE))
    qrep[...] = q
    s_self = head_sums(q * column(kn_ref))
    v_new = column(vn_ref)
    m_run[...] = s_self
    l_run[...] = jnp.zeros(stat, F32)
    acc_run[...] = jnp.zeros_like(acc_run)
    tail[...] = jnp.broadcast_to(column(lfn_ref), stat)

    def group_step(g, carry):
        slot = lax.rem(first + g, 2)

        @pl.when(g + 1 < ng)
        def _():
            for c in page_copies(s, g + 1, 1 - slot):
                c.start()

        @pl.when(jnp.logical_and(g + 1 == ng, s + 1 < pl.num_programs(0)))
        def _():
            for c in page_copies(s + 1, 0, 1 - slot):
                c.start()

        for c in page_copies(s, g, slot):
            c.wait()

        qv = qrep[...]
        rows = [lfbuf[slot, r] for r in range(n_pages)]
        rows.append(jnp.zeros((LANE - n_pages * SUBLANE, LANE), F32))
        lf = jnp.concatenate(rows, axis=0)
        suf = _dot3_lhs(lf, msuf_ref[...])
        tot = _dot3_lhs(lf, ones_ref[...])
        pre = _dot3_rhs(pex_ref[...], tot)
        t_in = tail[...]
        last = slice((n_pages - 1) * SUBLANE, n_pages * SUBLANE)
        tail[...] = t_in + pre[last] + tot[last]
        s_pages = []
        m_step = jnp.full((SUBLANE, 1), NEG_INF, F32)
        for r in range(n_pages):
            pr = slice(r * SUBLANE, (r + 1) * SUBLANE)
            sr = head_sums(kbuf[slot, r] * qv) + (suf[pr] + pre[pr] + t_in)
            s_pages.append(sr)
            m_step = jnp.maximum(m_step, jnp.max(sr, axis=1, keepdims=True))
        m_prev = m_run[...]
        m_new = jnp.maximum(m_prev, m_step)
        alpha = jnp.exp(m_prev - m_new)
        p_pages = [jnp.exp(sr - m_new) for sr in s_pages]
        l_step = jnp.zeros((SUBLANE, 1), F32)
        for p in p_pages:
            l_step = l_step + jnp.sum(p, axis=1, keepdims=True)
        for h in range(H_F):
            hr = slice(h * DH_F, (h + 1) * DH_F)
            acc = acc_run[hr, :] * alpha[h:h + 1, :]
            for r in range(n_pages):
                acc = acc + vbuf[slot, r, hr, :] * p_pages[r][h:h + 1, :]
            acc_run[hr, :] = acc
        l_run[...] = alpha * l_run[...] + l_step
        m_run[...] = m_new
        return carry

    lax.fori_loop(0, ng, group_step, 0)
    count[0] = first + ng

    w_self = jnp.exp(s_self - m_run[...])
    l_tot = l_run[...] + w_self
    for h in range(H_F):
        hr = slice(h * DH_F, (h + 1) * DH_F)
        acc_col = jnp.sum(acc_run[hr, :], axis=1, keepdims=True)
        o_ref[0, hr, :] = (acc_col + w_self[h:h + 1, :1] * v_new[hr, :]) / l_tot[h:h + 1, :1]


def _fox_decode(page_table, fq_t, fk_t, fv_t, lfn_t, ck_t, cv_t, clf_t, thr):
    ns, n_tab = page_table.shape
    pg = min(DEC_PAGES, n_tab)
    assert pg * SUBLANE <= LANE and n_tab % pg == 0

    tot = _page_totals(clf_t)[page_table]
    after = jnp.transpose(lfn_t[0])[:, None, :] + jnp.cumsum(tot[:, ::-1], axis=1)[:, ::-1] - tot
    need = jnp.maximum(jnp.sum(jnp.any(after >= -thr, axis=-1), axis=1), 1)
    n_groups = ((need + pg - 1) // pg).astype(jnp.int32)

    r = np.arange(LANE)
    msuf = jnp.asarray((r[:, None] > r[None, :]).astype(np.float32), dtype=BF16)
    same_head = (r[:, None] % SUBLANE) == (r[None, :] % SUBLANE)
    in_step = (r[:, None] < pg * SUBLANE) & (r[None, :] < pg * SUBLANE)
    pex = jnp.asarray((same_head & in_step & (r[None, :] // SUBLANE < r[:, None] // SUBLANE)).astype(np.float32),
                      dtype=BF16)
    hbm = pl.BlockSpec(memory_space=pl.ANY)
    stat = pltpu.VMEM((SUBLANE, LANE), F32)
    grid_spec = pltpu.PrefetchScalarGridSpec(
        num_scalar_prefetch=2,
        grid=(ns,),
        in_specs=[_const_spec((1, D_FQK, ns)), _const_spec((1, D_FQK, ns)), _const_spec((1, D_FQK, ns)),
                  _const_spec((1, H_F, ns)), _const_spec((LANE, LANE)),
                  _const_spec((LANE, LANE)), _const_spec((LANE, LANE)), hbm, hbm, hbm],
        out_specs=pl.BlockSpec((1, D_FQK, 1), lambda s, pt, ng: (s, 0, 0)),
        scratch_shapes=[pltpu.VMEM((2, pg, D_FQK, PAGE_SIZE), F32), pltpu.VMEM((2, pg, D_FQK, PAGE_SIZE), F32),
                        pltpu.VMEM((2, pg, H_F, PAGE_SIZE), F32), pltpu.SemaphoreType.DMA((2, 3)),
                        stat, stat, stat, pltpu.VMEM((D_FQK, LANE), F32), pltpu.VMEM((D_FQK, LANE), F32),
                        pltpu.SMEM((1,), jnp.int32)],
    )
    return pl.pallas_call(
        functools.partial(_fox_decode_body, n_pages=pg, n_tab=n_tab),
        out_shape=jax.ShapeDtypeStruct((ns, D_FQK, 1), F32),
        grid_spec=grid_spec,
        compiler_params=_cparams(("arbitrary",)),
        name="fox_decode",
    )(page_table, n_groups, fq_t, fk_t, fv_t, lfn_t, msuf, jnp.ones((LANE, LANE), BF16), pex,
      ck_t, cv_t, clf_t)


def _outproj_body(x_ref, gt_ref, or_ref, of_ref, gr_ref, gf_ref, wb_ref, wo_ref, o_ref, *, per_row):
    gt = _mod_rows(gt_ref, per_row)
    merged = (gr_ref[0].astype(F32) * _dot(or_ref[0], wb_ref[:D_RV, :])
              + gf_ref[0].astype(F32) * _dot(of_ref[0], wb_ref[D_RV:, :]))
    y = _dot(merged.astype(BF16), wo_ref[...])
    o_ref[0] = x_ref[0] + gt * y


def _outproj(x, mod, per_row, row_off, o_r, o_f, gr, gf, w_branch, w_out):
    nb, nt, _ = x.shape
    tm = min(FFN_TILE, nt)
    row = lambda w: pl.BlockSpec((1, tm, w), lambda b, t: (b, t, 0))
    return pl.pallas_call(
        functools.partial(_outproj_body, per_row=per_row),
        out_shape=jax.ShapeDtypeStruct(x.shape, F32),
        grid=(nb, nt // tm),
        in_specs=[row(D_MODEL), _mod_specs(1, per_row, nt, row_off)[2], row(D_RV), row(D_FQK),
                  row(D_MODEL), row(D_MODEL), _const_spec((D_RV + D_FQK, D_MODEL)),
                  _const_spec((D_MODEL, D_MODEL))],
        out_specs=row(D_MODEL),
        compiler_params=_cparams(("parallel", "parallel")),
        name="outproj",
    )(x, mod, o_r, o_f, gr, gf, w_branch, w_out)


def _rope_tables(pos):
    half = DK_R // 2
    inv = ROPE_BASE ** (-jnp.arange(half, dtype=F32) / half)
    ang = pos.astype(F32)[:, None] * inv[None, :]
    cos, sin = jnp.cos(ang), jnp.sin(ang)
    return jnp.concatenate([cos, cos], axis=1), jnp.concatenate([-sin, sin], axis=1)


def _permute_w_in(w_in):
    sizes = (D_RQK, D_RQK, D_RV, D_RV, D_FQK, D_FQK, D_FQK, H_F, D_MODEL, D_MODEL)
    offs = [0]
    for s in sizes:
        offs.append(offs[-1] + s)
    part = lambda i: w_in[:, offs[i]:offs[i + 1]]
    fl = jnp.pad(part(7), ((0, 0), (0, LANE - H_F)))
    return jnp.concatenate([part(i) for i in (0, 1, 2, 3, 4, 5, 6, 8, 9)] + [fl], axis=1).astype(BF16)


def kernel(x_prompt, x_sample, cache_k, cache_v, cache_logf, state_ret, page_table, c_prompt, c_sample,
           w_ada, b_ada, w_ffa_up, w_ffa_down, w_in, b_forget, q_norm_w, k_norm_w, ret_norm_w,
           w_branch, w_out, w_ffb_up, w_ffb_down):
    nb, nt, _ = x_prompt.shape
    ns = x_sample.shape[0]
    depth = w_ada.shape[0]
    past_len = page_table.shape[1] * PAGE_SIZE
    assert x_sample.shape[1] == 1 and nb <= SUBLANE and ns % SUBLANE == 0 and nt % ATT_BLK == 0

    pad = (-(ns + SUBLANE)) % (2 * SUBLANE)
    c_all = jnp.concatenate([c_sample, c_prompt, jnp.zeros((SUBLANE - nb + pad, D_MODEL), F32)], axis=0)
    cos_p, sin_p = _rope_tables(jnp.arange(nt, dtype=jnp.int32))
    cos_s, sin_s = _rope_tables(jnp.full((ns,), past_len, jnp.int32))
    hd = np.arange(D_FQK) // DH_F
    bd = jnp.asarray((hd[:, None] == hd[None, :]).astype(np.float32) / DH_F, dtype=BF16)

    y_p = x_prompt
    y_s = x_sample.reshape(1, ns, D_MODEL)
    outs = [[] for _ in range(8)]
    for l in range(depth):
        bf = lambda w: w[l].astype(BF16)
        w_up_a, w_dn_a, w_up_b, w_dn_b = bf(w_ffa_up), bf(w_ffa_down), bf(w_ffb_up), bf(w_ffb_down)
        w_br, w_o = bf(w_branch), bf(w_out)
        w_inp = _permute_w_in(w_in[l])
        bf_row = jnp.pad(b_forget[l], (0, LANE - H_F)).reshape(1, LANE)
        qn = jnp.tile(q_norm_w[l], H_F).reshape(1, D_FQK)
        kn = jnp.tile(k_norm_w[l], H_F).reshape(1, D_FQK)
        rn = ret_norm_w[l].reshape(1, D_RV)
        mod = _adaln(c_all, w_ada[l], b_ada[l])
        qk_bound = 1.02 * DH_F ** 0.5 * jnp.max(jnp.abs(q_norm_w[l])) * jnp.max(jnp.abs(k_norm_w[l]))
        thr = EXP_UNDERFLOW + 2.0 * qk_bound
        page_view = lambda c, w: jnp.transpose(c, (0, 2, 3, 1)).reshape(c.shape[0], w, PAGE_SIZE)

        y_p = _ffn(y_p, mod, 0, False, ns, w_up_a, w_dn_a)
        pj = _inproj(y_p, mod, False, ns, cos_p, sin_p, w_inp, bf_row, qn, kn, bd, fused=(rn, thr))
        skip = pj["skip"][:, :, 0, :H_F].astype(jnp.int32).reshape(-1)
        o_f = _fox_prompt(skip, pj["fq"], pj["qa"], pj["fkb"], pj["ka"], pj["fvb"], pj["base"], ATT_BLK)
        st_p = pj["st"]
        y_p = _outproj(y_p, mod, False, ns, pj["o_r"], o_f, pj["gr"], pj["gf"], w_br, w_o)
        y_p = _ffn(y_p, mod, 2, False, ns, w_up_b, w_dn_b)

        y_s = _ffn(y_s, mod, 0, True, 0, w_up_a, w_dn_a)
        sj = _inproj(y_s, mod, True, 0, cos_s, sin_s, w_inp, bf_row, qn, kn, bd)
        seq = lambda a: a.reshape(ns, 1, a.shape[-1])
        o_f_s = _fox_decode(page_table, sj["fq"], sj["fkt"], sj["fvt"], sj["lft"],
                            page_view(cache_k[l], D_FQK), page_view(cache_v[l], D_FQK),
                            jnp.transpose(cache_logf[l], (0, 2, 1)), thr)
        o_r_s, st_s = _ret_decode(seq(sj["rq"]), seq(sj["rk"]), seq(sj["rv"]), seq(sj["rg"]), rn, state_ret[l])
        y_s = _outproj(y_s, mod, True, 0, o_r_s.reshape(1, ns, D_RV), o_f_s.reshape(1, ns, D_FQK).astype(BF16),
                       sj["gr"], sj["gf"], w_br, w_o)
        y_s = _ffn(y_s, mod, 2, True, 0, w_up_b, w_dn_b)

        rows = lambda a: jnp.transpose(a.reshape(a.shape[0], H_F, DH_F, a.shape[2]), (0, 3, 1, 2))
        for dst, val in zip(outs, (rows(pj["fkt"]), rows(pj["fvt"]), jnp.transpose(pj["lft"], (0, 2, 1)), st_p,
                                   rows(sj["fkt"]).reshape(ns, 1, H_F, DH_F),
                                   rows(sj["fvt"]).reshape(ns, 1, H_F, DH_F),
                                   jnp.transpose(sj["lft"], (0, 2, 1)).reshape(ns, 1, H_F), st_s)):
            dst.append(val)

    return (y_p, y_s.reshape(ns, 1, D_MODEL), *[jnp.stack(o) for o in outs])
```
